```python
import math
import jax, jax.numpy as jnp
from jax import lax
import numpy as np

D_MODEL = 1024
BATCH = 16
SEQ = 2048
DEPTH = 2

N_MIXERS = 2
N_CONV = (DEPTH + 1) // 2
N_ATTN = DEPTH // 2
CONV_WIDTH = 31
N_HEADS = 16
HEAD_DIM = D_MODEL // N_HEADS
BLOCK_Q = 128
PEER_HEADS = 8
PEER_DQ = 256
PEER_DQH = PEER_DQ // 2
N_KEYS = 128
N_EXPERTS = N_KEYS * N_KEYS
PEER_TOPK = 16
PEER_CHUNK = 128
ADA_CHUNKS = 6
EPS = 1e-6

kernel_name = 'hybrid_conv_stickbreak_peer_adaln'


def _rmsnorm(x, g):
    x32 = x.astype(jnp.float32)
    y = x32 * lax.rsqrt(jnp.mean(x32 * x32, axis=-1, keepdims=True) + EPS)
    return y.astype(x.dtype) * g


def _layernorm(x, g, b):
    x32 = x.astype(jnp.float32)
    mu = jnp.mean(x32, axis=-1, keepdims=True)
    var = jnp.mean(jnp.square(x32 - mu), axis=-1, keepdims=True)
    y = (x32 - mu) * lax.rsqrt(var + EPS)
    return y.astype(x.dtype) * g + b


def _conformer_conv(h, w_in, b_in, w_dw, b_dw, ln_g, ln_b, w_out, b_out):
    a, gate = jnp.split(h @ w_in + b_in, 2, axis=-1)
    u = a * jax.nn.sigmoid(gate)
    u = lax.conv_general_dilated(
        u, w_dw[:, None, :], window_strides=(1,),
        padding=[(CONV_WIDTH - 1, 0)],
        dimension_numbers=('NWC', 'WIO', 'NWC'),
        feature_group_count=D_MODEL) + b_dw
    u = jax.nn.silu(_layernorm(u, ln_g, ln_b))
    return u @ w_out + b_out


def _stick_breaking_attention(h, w_qkv, w_o):
    B, S, _ = h.shape
    qkv = (h @ w_qkv).reshape(B, S, 3, N_HEADS, HEAD_DIM)
    q = jnp.transpose(qkv[:, :, 0], (0, 2, 1, 3))
    k = jnp.transpose(qkv[:, :, 1], (0, 2, 1, 3))
    v = jnp.transpose(qkv[:, :, 2], (0, 2, 1, 3))
    scale = 1.0 / math.sqrt(HEAD_DIM)
    outs = []
    for blk in range(S // BLOCK_Q):
        q0 = blk * BLOCK_Q
        kend = q0 + BLOCK_Q
        qb = q[:, :, q0:kend]
        kb = k[:, :, :kend]
        vb = v[:, :, :kend]
        z = jnp.einsum('bhqd,bhkd->bhqk', qb, kb).astype(jnp.float32) * scale
        t_idx = q0 + jnp.arange(BLOCK_Q)[:, None]
        s_idx = jnp.arange(kend)[None, :]
        mask = s_idx < t_idx
        log_beta = jax.nn.log_sigmoid(z)
        log_1m = jnp.where(mask, -jax.nn.softplus(z), 0.0)
        log_w = log_beta + lax.cumsum(log_1m, axis=3, reverse=True) - log_1m
        a = jnp.where(mask, jnp.exp(log_w), 0.0).astype(vb.dtype)
        outs.append(jnp.einsum('bhqk,bhkd->bhqd', a, vb))
    o = jnp.concatenate(outs, axis=2)
    o = jnp.transpose(o, (0, 2, 1, 3)).reshape(B, S, N_HEADS * HEAD_DIM)
    return o @ w_o


def _peer(h, w_q, sub_keys, u_tab, v_tab):
    B, S, D = h.shape
    T = B * S
    hf = h.reshape(T, D)
    q = (hf @ w_q).reshape(T, PEER_HEADS, 2, PEER_DQH)
    s = jnp.einsum('thcd,hcnd->thcn', q, sub_keys).astype(jnp.float32)
    top_s, top_i = lax.top_k(s, PEER_TOPK)
    cand = (top_s[:, :, 0, :, None] + top_s[:, :, 1, None, :]).reshape(
        T, PEER_HEADS, PEER_TOPK * PEER_TOPK)
    best_s, best_c = lax.top_k(cand, PEER_TOPK)
    i1 = jnp.take_along_axis(top_i[:, :, 0], best_c // PEER_TOPK, axis=-1)
    i2 = jnp.take_along_axis(top_i[:, :, 1], best_c % PEER_TOPK, axis=-1)
    experts = i1 * N_KEYS + i2
    gates = jax.nn.softmax(best_s, axis=-1).astype(h.dtype)
    n_chunk = T // PEER_CHUNK
    hk = PEER_HEADS * PEER_TOPK
    xs = hf.reshape(n_chunk, PEER_CHUNK, D)
    es = experts.reshape(n_chunk, PEER_CHUNK, hk)
    gs = gates.reshape(n_chunk, PEER_CHUNK, hk)

    def chunk(args):
        xc, ec, gc = args
        u_sel = u_tab[ec]
        act = jax.nn.gelu(jnp.einsum('tkd,td->tk', u_sel, xc), approximate=False) * gc
        return jnp.einsum('tk,tkd->td', act, v_tab[ec])

    y = lax.map(chunk, (xs, es, gs))
    return y.reshape(B, S, D)


def setup_inputs(seed: int = 0) -> dict:
    key = jax.random.key(seed)
    ks = jax.random.split(key, 24)
    D = D_MODEL
    n = jax.random.normal
    f = jnp.float32
    inv = D ** -0.5
    return {
        'x': n(ks[0], (BATCH, SEQ, D), f),
        'c': n(ks[1], (BATCH, D), f),
        'ada_w': n(ks[2], (DEPTH, D, ADA_CHUNKS * D), f) * (0.5 * inv),
        'ada_b': n(ks[3], (DEPTH, ADA_CHUNKS * D), f) * 0.01,
        'norm_mix_g': 1.0 + 0.02 * n(ks[4], (DEPTH, D), f),
        'norm_ffn_g': 1.0 + 0.02 * n(ks[5], (DEPTH, D), f),
        'cv_w_in': n(ks[6], (N_CONV, D, 2 * D), f) * inv,
        'cv_b_in': n(ks[7], (N_CONV, 2 * D), f) * 0.01,
        'cv_w_dw': n(ks[8], (N_CONV, CONV_WIDTH, D), f) * CONV_WIDTH ** -0.5,
        'cv_b_dw': n(ks[9], (N_CONV, D), f) * 0.01,
        'cv_ln_g': 1.0 + 0.02 * n(ks[10], (N_CONV, D), f),
        'cv_ln_b': n(ks[11], (N_CONV, D), f) * 0.01,
        'cv_w_out': n(ks[12], (N_CONV, D, D), f) * inv,
        'cv_b_out': n(ks[13], (N_CONV, D), f) * 0.01,
        'sb_w_qkv': n(ks[14], (N_ATTN, D, 3 * N_HEADS * HEAD_DIM), f) * inv,
        'sb_w_o': n(ks[15], (N_ATTN, N_HEADS * HEAD_DIM, D), f) * (N_HEADS * HEAD_DIM) ** -0.5,
        'pk_w_q': n(ks[16], (DEPTH, D, PEER_HEADS * PEER_DQ), f) * inv,
        'pk_keys': n(ks[17], (DEPTH, PEER_HEADS, 2, N_KEYS, PEER_DQH), f) * PEER_DQH ** -0.5,
        'pk_u': n(ks[18], (DEPTH, N_EXPERTS, D), f) * inv,
        'pk_v': n(ks[19], (DEPTH, N_EXPERTS, D), f),
        'final_g': 1.0 + 0.02 * n(ks[20], (D,), f),
    }


def reference(x, c, ada_w, ada_b, norm_mix_g, norm_ffn_g, cv_w_in, cv_b_in, cv_w_dw,
              cv_b_dw, cv_ln_g, cv_ln_b, cv_w_out, cv_b_out, sb_w_qkv, sb_w_o,
              pk_w_q, pk_keys, pk_u, pk_v, final_g):
    c_act = jax.nn.silu(c)
    for i in range(DEPTH):
        mod = (c_act @ ada_w[i] + ada_b[i])[:, None, :]
        sh1, sc1, g1, sh2, sc2, g2 = jnp.split(mod, ADA_CHUNKS, axis=-1)
        h = _rmsnorm(x, norm_mix_g[i]) * (1.0 + sc1) + sh1
        j = i // N_MIXERS
        if i % N_MIXERS == 0:
            m = _conformer_conv(h, cv_w_in[j], cv_b_in[j], cv_w_dw[j], cv_b_dw[j],
                                cv_ln_g[j], cv_ln_b[j], cv_w_out[j], cv_b_out[j])
        else:
            m = _stick_breaking_attention(h, sb_w_qkv[j], sb_w_o[j])
        x = x + g1 * m
        h = _rmsnorm(x, norm_ffn_g[i]) * (1.0 + sc2) + sh2
        x = x + g2 * _peer(h, pk_w_q[i], pk_keys[i], pk_u[i], pk_v[i])
    return _rmsnorm(x, final_g)
```

```python
import math
import functools

import jax
import jax.numpy as jnp
from jax import lax
from jax.experimental import pallas as pl
from jax.experimental.pallas import tpu as pltpu

D_MODEL = 1024
DEPTH = 2
N_MIXERS = 2
CONV_WIDTH = 31
N_HEADS = 16
HEAD_DIM = 64
BLOCK_Q = 128
PEER_HEADS = 8
PEER_DQ = 256
PEER_DQH = 128
N_KEYS = 128
PEER_TOPK = 16
PEER_CHUNK = 128
ADA_CHUNKS = 6
EPS = 1e-6


def _rmsnorm(x, g):
    y = x * lax.rsqrt(jnp.mean(x * x, axis=-1, keepdims=True) + EPS)
    return y * g


def _layernorm(x, g, b):
    mu = jnp.mean(x, axis=-1, keepdims=True)
    var = jnp.mean(jnp.square(x - mu), axis=-1, keepdims=True)
    return (x - mu) * lax.rsqrt(var + EPS) * g + b


def _conv_module(h, w_in, b_in, w_dw, b_dw, ln_g, ln_b, w_out, b_out):
    a, gate = jnp.split(h @ w_in + b_in, 2, axis=-1)
    u = a * jax.nn.sigmoid(gate)
    u = lax.conv_general_dilated(
        u, w_dw[:, None, :], window_strides=(1,),
        padding=[(CONV_WIDTH - 1, 0)],
        dimension_numbers=('NWC', 'WIO', 'NWC'),
        feature_group_count=D_MODEL) + b_dw
    u = jax.nn.silu(_layernorm(u, ln_g, ln_b))
    return u @ w_out + b_out


def _sb_attention(h, w_qkv, w_o):
    B, S, _ = h.shape
    qkv = (h @ w_qkv).reshape(B, S, 3, N_HEADS, HEAD_DIM)
    q = jnp.transpose(qkv[:, :, 0], (0, 2, 1, 3))
    k = jnp.transpose(qkv[:, :, 1], (0, 2, 1, 3))
    v = jnp.transpose(qkv[:, :, 2], (0, 2, 1, 3))
    scale = 1.0 / math.sqrt(HEAD_DIM)
    outs = []
    for blk in range(S // BLOCK_Q):
        q0 = blk * BLOCK_Q
        kend = q0 + BLOCK_Q
        qb = q[:, :, q0:kend]
        kb = k[:, :, :kend]
        vb = v[:, :, :kend]
        z = jnp.einsum('bhqd,bhkd->bhqk', qb, kb) * scale
        t_idx = q0 + jnp.arange(BLOCK_Q)[:, None]
        s_idx = jnp.arange(kend)[None, :]
        mask = s_idx < t_idx
        log_beta = jax.nn.log_sigmoid(z)
        log_1m = jnp.where(mask, -jax.nn.softplus(z), 0.0)
        log_w = log_beta + lax.cumsum(log_1m, axis=3, reverse=True) - log_1m
        a = jnp.where(mask, jnp.exp(log_w), 0.0)
        outs.append(jnp.einsum('bhqk,bhkd->bhqd', a, vb))
    o = jnp.concatenate(outs, axis=2)
    o = jnp.transpose(o, (0, 2, 1, 3)).reshape(B, S, N_HEADS * HEAD_DIM)
    return o @ w_o


def _peer(h, w_q, sub_keys, u_tab, v_tab):
    B, S, D = h.shape
    T = B * S
    hf = h.reshape(T, D)
    q = (hf @ w_q).reshape(T, PEER_HEADS, 2, PEER_DQH)
    s = jnp.einsum('thcd,hcnd->thcn', q, sub_keys)
    top_s, top_i = lax.top_k(s, PEER_TOPK)
    cand = (top_s[:, :, 0, :, None] + top_s[:, :, 1, None, :]).reshape(
        T, PEER_HEADS, PEER_TOPK * PEER_TOPK)
    best_s, best_c = lax.top_k(cand, PEER_TOPK)
    i1 = jnp.take_along_axis(top_i[:, :, 0], best_c // PEER_TOPK, axis=-1)
    i2 = jnp.take_along_axis(top_i[:, :, 1], best_c % PEER_TOPK, axis=-1)
    experts = i1 * N_KEYS + i2
    gates = jax.nn.softmax(best_s, axis=-1)
    n_chunk = T // PEER_CHUNK
    hk = PEER_HEADS * PEER_TOPK
    xs = hf.reshape(n_chunk, PEER_CHUNK, D)
    es = experts.reshape(n_chunk, PEER_CHUNK, hk)
    gs = gates.reshape(n_chunk, PEER_CHUNK, hk)

    def chunk(args):
        xc, ec, gc = args
        u_sel = u_tab[ec]
        act = jax.nn.gelu(jnp.einsum('tkd,td->tk', u_sel, xc), approximate=False) * gc
        return jnp.einsum('tk,tkd->td', act, v_tab[ec])

    y = lax.map(chunk, (xs, es, gs))
    return y.reshape(B, S, D)


def _final_norm_kernel(x_ref, g_ref, o_ref):
    x = x_ref[...]
    ms = jnp.mean(x * x, axis=-1, keepdims=True)
    o_ref[...] = x * lax.rsqrt(ms + EPS) * g_ref[...]


def _final_norm(x2d, g):
    T, D = x2d.shape
    tm = 1024
    return pl.pallas_call(
        _final_norm_kernel,
        grid=(T // tm,),
        in_specs=[pl.BlockSpec((tm, D), lambda i: (i, 0)),
                  pl.BlockSpec((1, D), lambda i: (0, 0))],
        out_specs=pl.BlockSpec((tm, D), lambda i: (i, 0)),
        out_shape=jax.ShapeDtypeStruct((T, D), x2d.dtype),
    )(x2d, g.reshape(1, D))


def kernel(x, c, ada_w, ada_b, norm_mix_g, norm_ffn_g, cv_w_in, cv_b_in, cv_w_dw, cv_b_dw, cv_ln_g, cv_ln_b, cv_w_out, cv_b_out, sb_w_qkv, sb_w_o, pk_w_q, pk_keys, pk_u, pk_v, final_g):
    B, S, D = x.shape
    c_act = jax.nn.silu(c)
    for i in range(DEPTH):
        mod = (c_act @ ada_w[i] + ada_b[i])[:, None, :]
        sh1, sc1, g1, sh2, sc2, g2 = jnp.split(mod, ADA_CHUNKS, axis=-1)
        h = _rmsnorm(x, norm_mix_g[i]) * (1.0 + sc1) + sh1
        j = i // N_MIXERS
        if i % N_MIXERS == 0:
            m = _conv_module(h, cv_w_in[j], cv_b_in[j], cv_w_dw[j], cv_b_dw[j],
                             cv_ln_g[j], cv_ln_b[j], cv_w_out[j], cv_b_out[j])
        else:
            m = _sb_attention(h, sb_w_qkv[j], sb_w_o[j])
        x = x + g1 * m
        h = _rmsnorm(x, norm_ffn_g[i]) * (1.0 + sc2) + sh2
        x = x + g2 * _peer(h, pk_w_q[i], pk_keys[i], pk_u[i], pk_v[i])
    return _final_norm(x.reshape(B * S, D), final_g).reshape(B, S, D)
```

```python
import functools
import math

import jax
import jax.numpy as jnp
from jax import lax
from jax.experimental import pallas as pl
from jax.experimental.pallas import tpu as pltpu
from jax.experimental.pallas import tpu_sc as plsc

D_MODEL = 1024
DEPTH = 2
CONV_WIDTH = 31
N_HEADS = 16
HEAD_DIM = 64
PEER_HEADS = 8
PEER_DQH = 128
N_KEYS = 128
PEER_TOPK = 16
PEER_HK = PEER_HEADS * PEER_TOPK
ADA_CHUNKS = 6
EPS = 1e-6

LANES = 128
SUBLANES = 8
SC_LANES = 16
VMEM_LIMIT = 48 * 1024 * 1024

F32 = jnp.float32
BF16 = jnp.bfloat16
NEG_INF = float("-inf")


def _cparams(*sem):
    return pltpu.CompilerParams(dimension_semantics=sem, vmem_limit_bytes=VMEM_LIMIT)


def _norm_mod(x, gn, sc, sh):
    ms = jnp.mean(x * x, axis=-1, keepdims=True)
    return (x * lax.rsqrt(ms + EPS)) * gn * (1.0 + sc) + sh


def _ada_kernel(c_ref, w_ref, b_ref, o_ref):
    c = c_ref[...]
    ca = c * jax.nn.sigmoid(c)
    o_ref[0] = jnp.dot(ca.astype(BF16), w_ref[0].astype(BF16),
                       preferred_element_type=F32) + b_ref[0]


def _ada_mod(c, ada_w, ada_b):
    depth, d, n = ada_w.shape
    bsz = c.shape[0]
    tn = 512
    return pl.pallas_call(
        _ada_kernel,
        grid=(depth, n // tn),
        in_specs=[pl.BlockSpec((bsz, d), lambda i, j: (0, 0)),
                  pl.BlockSpec((1, d, tn), lambda i, j: (i, 0, j)),
                  pl.BlockSpec((1, 1, tn), lambda i, j: (i, 0, j))],
        out_specs=pl.BlockSpec((1, bsz, tn), lambda i, j: (i, 0, j)),
        out_shape=jax.ShapeDtypeStruct((depth, bsz, n), F32),
        compiler_params=_cparams("parallel", "parallel"),
        name="ada_mod",
    )(c, ada_w, ada_b.reshape(depth, 1, n))


def _norm_mm_kernel(x_ref, gn_ref, sc_ref, sh_ref, w_ref, o_ref):
    h = _norm_mod(x_ref[...], gn_ref[...], sc_ref[0], sh_ref[0])
    o_ref[...] = jnp.dot(h.astype(BF16), w_ref[...], preferred_element_type=F32)


def _norm_mm(x2, seq, gn, sc, sh, w_bf16, tm=256):
    t, d = x2.shape
    n = w_bf16.shape[1]
    per_seq = lambda i: ((i * tm) // seq, 0, 0)
    return pl.pallas_call(
        _norm_mm_kernel,
        grid=(t // tm,),
        in_specs=[pl.BlockSpec((tm, d), lambda i: (i, 0)),
                  pl.BlockSpec((1, d), lambda i: (0, 0)),
                  pl.BlockSpec((1, 1, d), per_seq),
                  pl.BlockSpec((1, 1, d), per_seq),
                  pl.BlockSpec((d, n), lambda i: (0, 0))],
        out_specs=pl.BlockSpec((tm, n), lambda i: (i, 0)),
        out_shape=jax.ShapeDtypeStruct((t, n), F32),
        compiler_params=_cparams("parallel"),
        name="norm_mm",
    )(x2, gn.reshape(1, d), sc, sh, w_bf16)


def _norm_glu_kernel(x_ref, gn_ref, sc_ref, sh_ref, wa_ref, wg_ref, ba_ref, bg_ref, o_ref):
    h = _norm_mod(x_ref[...], gn_ref[...], sc_ref[0], sh_ref[0]).astype(BF16)
    a = jnp.dot(h, wa_ref[...], preferred_element_type=F32) + ba_ref[...]
    g = jnp.dot(h, wg_ref[...], preferred_element_type=F32) + bg_ref[...]
    o_ref[...] = a * jax.nn.sigmoid(g)


def _norm_glu(x2, seq, gn, sc, sh, w_in_bf16, b_in, tm=256):
    t, d = x2.shape
    n = w_in_bf16.shape[1] // 2
    per_seq = lambda i: ((i * tm) // seq, 0, 0)
    b2 = b_in.reshape(1, 2 * n)
    return pl.pallas_call(
        _norm_glu_kernel,
        grid=(t // tm,),
        in_specs=[pl.BlockSpec((tm, d), lambda i: (i, 0)),
                  pl.BlockSpec((1, d), lambda i: (0, 0)),
                  pl.BlockSpec((1, 1, d), per_seq),
                  pl.BlockSpec((1, 1, d), per_seq),
                  pl.BlockSpec((d, n), lambda i: (0, 0)),
                  pl.BlockSpec((d, n), lambda i: (0, 1)),
                  pl.BlockSpec((1, n), lambda i: (0, 0)),
                  pl.BlockSpec((1, n), lambda i: (0, 1))],
        out_specs=pl.BlockSpec((tm, n), lambda i: (i, 0)),
        out_shape=jax.ShapeDtypeStruct((t, n), F32),
        compiler_params=_cparams("parallel"),
        name="norm_glu",
    )(x2, gn.reshape(1, d), sc, sh, w_in_bf16, w_in_bf16, b2, b2)


HALO = 32


def _conv_kernel(cur_ref, prev_ref, wdw_ref, bdw_ref, lng_ref, lnb_ref, wout_ref,
                 bout_ref, x_ref, g_ref, o_ref, win_s):
    ts = cur_ref.shape[1]
    first = pl.program_id(1) == 0
    win_s[0:HALO, :] = jnp.where(first, 0.0, prev_ref[0])
    win_s[HALO:HALO + ts, :] = cur_ref[0]
    acc = jnp.zeros(cur_ref.shape[1:], F32) + bdw_ref[...]
    off = HALO - (CONV_WIDTH - 1)
    for k in range(CONV_WIDTH):
        acc = acc + win_s[off + k:off + k + ts, :] * wdw_ref[k:k + 1, :]
    mu = jnp.mean(acc, axis=-1, keepdims=True)
    cen = acc - mu
    var = jnp.mean(cen * cen, axis=-1, keepdims=True)
    y = cen * lax.rsqrt(var + EPS) * lng_ref[...] + lnb_ref[...]
    y = y * jax.nn.sigmoid(y)
    m = jnp.dot(y.astype(BF16), wout_ref[...], preferred_element_type=F32) + bout_ref[...]
    o_ref[0] = x_ref[0] + g_ref[0] * m


def _conv_block(u3, x3, g1, w_dw, b_dw, ln_g, ln_b, w_out_bf16, b_out, ts=256):
    bsz, seq, d = u3.shape
    r = ts // HALO
    wdw = jnp.concatenate([w_dw, jnp.zeros((HALO - CONV_WIDTH, d), F32)], axis=0)
    vec = lambda b, i: (0, 0)
    return pl.pallas_call(
        _conv_kernel,
        grid=(bsz, seq // ts),
        in_specs=[pl.BlockSpec((1, ts, d), lambda b, i: (b, i, 0)),
                  pl.BlockSpec((1, HALO, d), lambda b, i: (b, jnp.maximum(i * r - 1, 0), 0)),
                  pl.BlockSpec((HALO, d), vec),
                  pl.BlockSpec((1, d), vec),
                  pl.BlockSpec((1, d), vec),
                  pl.BlockSpec((1, d), vec),
                  pl.BlockSpec((d, d), vec),
                  pl.BlockSpec((1, d), vec),
                  pl.BlockSpec((1, ts, d), lambda b, i: (b, i, 0)),
                  pl.BlockSpec((1, 1, d), lambda b, i: (b, 0, 0))],
        out_specs=pl.BlockSpec((1, ts, d), lambda b, i: (b, i, 0)),
        out_shape=jax.ShapeDtypeStruct((bsz, seq, d), F32),
        scratch_shapes=[pltpu.VMEM((HALO + ts, d), F32)],
        compiler_params=_cparams("parallel", "arbitrary"),
        name="conv_block",
    )(u3, u3, wdw, b_dw.reshape(1, d), ln_g.reshape(1, d), ln_b.reshape(1, d),
      w_out_bf16, b_out.reshape(1, d), x3, g1)


ATT_T = 128


def _split3(v):
    hi = v.astype(BF16)
    r1 = v - hi.astype(F32)
    mid = r1.astype(BF16)
    lo = (r1 - mid.astype(F32)).astype(BF16)
    return hi, mid, lo


def _attn_kernel(q_ref, k_ref, v_ref, o_ref):
    tq = ATT_T
    qi = pl.program_id(2)
    q2 = q_ref[0] * (1.0 / math.sqrt(HEAD_DIM))
    lane = lax.broadcasted_iota(jnp.int32, (1, LANES), 1)
    head_mask = (lane < HEAD_DIM, lane >= HEAD_DIM)
    row = lax.broadcasted_iota(jnp.int32, (tq, tq), 0)
    col = lax.broadcasted_iota(jnp.int32, (tq, tq), 1)
    tri = (row >= col).astype(BF16)
    qh = tuple(jnp.where(m, q2, 0.0).astype(BF16) for m in head_mask)
    nt = (((1,), (1,)), ((), ()))

    def tile(kt, suf, acc, diag):
        start = pl.multiple_of(kt * tq, tq)
        k2 = k_ref[0, pl.ds(start, tq), :].astype(BF16)
        v2 = v_ref[0, pl.ds(start, tq), :].astype(BF16)
        new_suf = []
        for h in range(2):
            z = lax.dot_general(qh[h], k2, nt, preferred_element_type=F32)
            lg = jnp.log1p(jnp.exp(-jnp.abs(z)))
            l1m = -(jnp.maximum(z, 0.0) + lg)
            if diag:
                l1m = jnp.where(col < row, l1m, 0.0)
            lb = jnp.minimum(z, 0.0) - lg
            hi, mid, lo = _split3(l1m)
            cs = (jnp.dot(hi, tri, preferred_element_type=F32)
                  + jnp.dot(mid, tri, preferred_element_type=F32)
                  + jnp.dot(lo, tri, preferred_element_type=F32))
            a = jnp.exp(lb + (cs - l1m) + suf[h])
            if diag:
                a = jnp.where(col < row, a, 0.0)
            pv = jnp.dot(a.astype(BF16), v2, preferred_element_type=F32)
            acc = acc + jnp.where(head_mask[h], pv, 0.0)
            new_suf.append(suf[h] + cs[:, 0:1])
        return tuple(new_suf), acc

    zero = jnp.zeros((tq, 1), F32)
    suf, acc = tile(qi, (zero, zero), jnp.zeros((tq, LANES), F32), True)

    def body(n, carry):
        s0, s1, acc = carry
        (s0, s1), acc = tile(qi - n, (s0, s1), acc, False)
        return s0, s1, acc

    _, _, acc = lax.fori_loop(1, qi + 1, body, (suf[0], suf[1], acc))
    o_ref[0] = acc


def _attention(qkv3):
    bsz, seq, n3 = qkv3.shape
    d = n3 // 3
    nb = d // LANES
    return pl.pallas_call(
        _attn_kernel,
        grid=(bsz, nb, seq // ATT_T),
        in_specs=[pl.BlockSpec((1, ATT_T, LANES), lambda b, h, i: (b, i, h)),
                  pl.BlockSpec((1, seq, LANES), lambda b, h, i: (b, 0, nb + h)),
                  pl.BlockSpec((1, seq, LANES), lambda b, h, i: (b, 0, 2 * nb + h))],
        out_specs=pl.BlockSpec((1, ATT_T, LANES), lambda b, h, i: (b, i, h)),
        out_shape=jax.ShapeDtypeStruct((bsz, seq, d), F32),
        compiler_params=_cparams("parallel", "parallel", "arbitrary"),
        name="sb_attention",
    )(qkv3, qkv3, qkv3)


def _mm_res_kernel(a_ref, w_ref, x_ref, g_ref, o_ref):
    m = jnp.dot(a_ref[...].astype(BF16), w_ref[...], preferred_element_type=F32)
    o_ref[...] = x_ref[...] + g_ref[0] * m


def _mm_res(a2, w_bf16, x2, g, seq, tm=512):
    t, d = x2.shape
    k = a2.shape[1]
    return pl.pallas_call(
        _mm_res_kernel,
        grid=(t // tm,),
        in_specs=[pl.BlockSpec((tm, k), lambda i: (i, 0)),
                  pl.BlockSpec((k, d), lambda i: (0, 0)),
                  pl.BlockSpec((tm, d), lambda i: (i, 0)),
                  pl.BlockSpec((1, 1, d), lambda i: ((i * tm) // seq, 0, 0))],
        out_specs=pl.BlockSpec((tm, d), lambda i: (i, 0)),
        out_shape=jax.ShapeDtypeStruct((t, d), F32),
        compiler_params=_cparams("parallel"),
        name="mm_res",
    )(a2, w_bf16, x2, g)


PAIR_ROWS = tuple(PEER_TOPK // (i + 1) for i in range(PEER_TOPK))


def _extract_top(s, payload, k):
    rows = lax.broadcasted_iota(jnp.int32, s.shape, 0)
    big = s.shape[0]
    vals, outs = [], []
    for _ in range(k):
        m = jnp.max(s, axis=0, keepdims=True)
        r = jnp.min(jnp.where(s == m, rows, big), axis=0, keepdims=True)
        sel = rows == r
        vals.append(m)
        if payload is None:
            outs.append(r)
        else:
            outs.append(jnp.max(jnp.where(sel, payload, -1), axis=0, keepdims=True))
        s = jnp.where(sel, NEG_INF, s)
    return jnp.concatenate(vals, axis=0), jnp.concatenate(outs, axis=0)


def _pair_candidates(a, ia, b, ib):
    sub = lax.broadcasted_iota(jnp.int32, (SUBLANES, a.shape[1]), 0)
    cand, ids = [], []
    for i in range(SUBLANES):
        for j0 in range(0, PAIR_ROWS[i], SUBLANES):
            n = min(PAIR_ROWS[i] - j0, SUBLANES)
            c = a[i:i + 1, :] + b[j0:j0 + SUBLANES, :]
            e = ia[i:i + 1, :] * N_KEYS + ib[j0:j0 + SUBLANES, :]
            if n < SUBLANES:
                c = jnp.where(sub < n, c, NEG_INF)
            cand.append(c)
            ids.append(e)
    cand.append(a[SUBLANES:, :] + b[0:1, :])
    ids.append(ia[SUBLANES:, :] * N_KEYS + ib[0:1, :])
    return jnp.concatenate(cand, axis=0), jnp.concatenate(ids, axis=0)


def _peer_route_kernel(x_ref, gn_ref, sc_ref, sh_ref, wq_ref, keys_ref,
                       h_ref, e_ref, g_ref, q_s, e_s, g_s):
    h = _norm_mod(x_ref[...], gn_ref[...], sc_ref[0], sh_ref[0])
    h_ref[...] = h
    q = jnp.dot(h.astype(BF16), wq_ref[...], preferred_element_type=F32)
    for j in range(2 * PEER_HEADS):
        q_s[j] = q[:, j * PEER_DQH:(j + 1) * PEER_DQH].astype(BF16)
    nt = (((1,), (1,)), ((), ()))

    def head(hd, carry):
        tops = []
        for c in range(2):
            s_t = lax.dot_general(keys_ref[2 * hd + c], q_s[2 * hd + c], nt,
                                  preferred_element_type=F32)
            tops.append(_extract_top(s_t, None, PEER_TOPK))
        cand, ids = _pair_candidates(tops[0][0], tops[0][1], tops[1][0], tops[1][1])
        best, experts = _extract_top(cand, ids, PEER_TOPK)
        ex = jnp.exp(best - best[0:1, :])
        g_s[hd] = ex / jnp.sum(ex, axis=0, keepdims=True)
        e_s[hd] = experts
        return carry

    lax.fori_loop(0, PEER_HEADS, head, 0)
    tm = x_ref.shape[0]
    e_ref[...] = e_s[...].reshape(PEER_HK, tm).T
    g_ref[...] = g_s[...].reshape(PEER_HK, tm).T


def _peer_route(x2, seq, gn, sc, sh, wq_bf16, keys_bf16, tm=256):
    t, d = x2.shape
    nq = wq_bf16.shape[1]
    per_seq = lambda i: ((i * tm) // seq, 0, 0)
    return pl.pallas_call(
        _peer_route_kernel,
        grid=(t // tm,),
        in_specs=[pl.BlockSpec((tm, d), lambda i: (i, 0)),
                  pl.BlockSpec((1, d), lambda i: (0, 0)),
                  pl.BlockSpec((1, 1, d), per_seq),
                  pl.BlockSpec((1, 1, d), per_seq),
                  pl.BlockSpec((d, nq), lambda i: (0, 0)),
                  pl.BlockSpec((2 * PEER_HEADS, N_KEYS, PEER_DQH), lambda i: (0, 0, 0))],
        out_specs=[pl.BlockSpec((tm, d), lambda i: (i, 0)),
                   pl.BlockSpec((tm, PEER_HK), lambda i: (i, 0)),
                   pl.BlockSpec((tm, PEER_HK), lambda i: (i, 0))],
        out_shape=[jax.ShapeDtypeStruct((t, d), F32),
                   jax.ShapeDtypeStruct((t, PEER_HK), jnp.int32),
                   jax.ShapeDtypeStruct((t, PEER_HK), F32)],
        scratch_shapes=[pltpu.VMEM((2 * PEER_HEADS, tm, PEER_DQH), BF16),
                        pltpu.VMEM((PEER_HEADS, PEER_TOPK, tm), jnp.int32),
                        pltpu.VMEM((PEER_HEADS, PEER_TOPK, tm), F32)],
        compiler_params=_cparams("parallel"),
        name="peer_route",
    )(x2, gn.reshape(1, d), sc, sh, wq_bf16, keys_bf16)


SC_ROWS = 16
SC_GROUPS = PEER_HK // SC_ROWS
SC_TB = 16
SC_CHUNKS = D_MODEL // SC_LANES


def _sc_mesh():
    return plsc.VectorSubcoreMesh(core_axis_name="c", subcore_axis_name="s")


def _sc_pipeline(n_items, gather, compute):
    gather(0, 0).start()

    @pl.loop(0, n_items, step=2)
    def _(it):
        gather(it, 0).wait()
        gather(it + 1, 1).start()
        compute(it, 0)
        gather(it + 1, 1).wait()

        @pl.when(it + 2 < n_items)
        def _():
            gather(it + 2, 0).start()

        compute(it + 1, 1)


def _peer_pre(h2, idx, u_tab):
    t, d = h2.shape
    info = plsc.get_sparse_core_info()
    nc, nw = info.num_cores, info.num_cores * info.num_subcores
    tpw = t // nw
    n_items = SC_TB * SC_GROUPS

    @functools.partial(
        pl.kernel, mesh=_sc_mesh(),
        out_type=jax.ShapeDtypeStruct((t, PEER_HK), F32),
        scratch_types=[pltpu.VMEM((SC_TB, PEER_HK), jnp.int32),
                       pltpu.VMEM((SC_TB, d), F32),
                       pltpu.VMEM((SC_TB, PEER_HK), F32),
                       pltpu.VMEM((SC_ROWS, d), F32),
                       pltpu.VMEM((SC_ROWS, d), F32),
                       pltpu.VMEM((SC_ROWS, SC_LANES), F32),
                       pltpu.SemaphoreType.DMA,
                       pltpu.SemaphoreType.DMA],
        compiler_params=pltpu.CompilerParams(needs_layout_passes=False),
        name="peer_pre")
    def k(h_hbm, idx_hbm, u_hbm, p_hbm, idx_v, h_v, p_v, rows0, rows1, tr_v, sem0, sem1):
        wid = lax.axis_index("s") * nc + lax.axis_index("c")
        bufs, sems = (rows0, rows1), (sem0, sem1)
        lane = lax.iota(jnp.int32, SC_LANES)

        def gather(it, b):
            tl, g = it // SC_GROUPS, it % SC_GROUPS
            return pltpu.make_async_copy(
                u_hbm.at[idx_v.at[tl, pl.ds(g * SC_ROWS, SC_ROWS)]], bufs[b], sems[b])

        def compute(it, b):
            tl, g = it // SC_GROUPS, it % SC_GROUPS
            buf = bufs[b]

            def body(c, accs):
                xc = h_v[tl, pl.ds(c * SC_LANES, SC_LANES)]
                return tuple(accs[r] + buf[r, pl.ds(c * SC_LANES, SC_LANES)] * xc
                             for r in range(SC_ROWS))

            zero = jnp.zeros((SC_LANES,), F32)
            accs = lax.fori_loop(0, SC_CHUNKS, body, (zero,) * SC_ROWS)
            for r in range(SC_ROWS):
                tr_v[r, :] = accs[r]
            res = zero
            for j in range(SC_LANES):
                res = res + plsc.load_gather(tr_v, [lane, jnp.full((SC_LANES,), j, jnp.int32)])
            p_v[tl, pl.ds(g * SC_ROWS, SC_ROWS)] = res

        @pl.loop(0, tpw // SC_TB)
        def _(blk):
            t0 = wid * tpw + blk * SC_TB
            pltpu.sync_copy(idx_hbm.at[pl.ds(t0, SC_TB)], idx_v)
            pltpu.sync_copy(h_hbm.at[pl.ds(t0, SC_TB)], h_v)
            _sc_pipeline(n_items, gather, compute)
            pltpu.sync_copy(p_v, p_hbm.at[pl.ds(t0, SC_TB)])

    return k(h2, idx, u_tab)


def _peer_post(act, idx, v_tab):
    t = act.shape[0]
    d = v_tab.shape[1]
    info = plsc.get_sparse_core_info()
    nc, nw = info.num_cores, info.num_cores * info.num_subcores
    tpw = t // nw
    n_items = SC_TB * SC_GROUPS

    @functools.partial(
        pl.kernel, mesh=_sc_mesh(),
        out_type=jax.ShapeDtypeStruct((t, d), F32),
        scratch_types=[pltpu.VMEM((SC_TB, PEER_HK), jnp.int32),
                       pltpu.VMEM((SC_TB, PEER_HK), F32),
                       pltpu.VMEM((SC_TB, d), F32),
                       pltpu.VMEM((SC_ROWS, d), F32),
                       pltpu.VMEM((SC_ROWS, d), F32),
                       pltpu.SemaphoreType.DMA,
                       pltpu.SemaphoreType.DMA],
        compiler_params=pltpu.CompilerParams(needs_layout_passes=False),
        name="peer_post")
    def k(a_hbm, idx_hbm, v_hbm, y_hbm, idx_v, a_v, y_v, rows0, rows1, sem0, sem1):
        wid = lax.axis_index("s") * nc + lax.axis_index("c")
        bufs, sems = (rows0, rows1), (sem0, sem1)

        def gather(it, b):
            tl, g = it // SC_GROUPS, it % SC_GROUPS
            return pltpu.make_async_copy(
                v_hbm.at[idx_v.at[tl, pl.ds(g * SC_ROWS, SC_ROWS)]], bufs[b], sems[b])

        def compute(it, b):
            tl, g = it // SC_GROUPS, it % SC_GROUPS
            buf = bufs[b]
            tlv = jnp.full((SC_LANES,), tl, jnp.int32)
            w = tuple(plsc.load_gather(a_v, [tlv, jnp.full((SC_LANES,), g * SC_ROWS + r, jnp.int32)])
                      for r in range(SC_ROWS))

            @pl.loop(0, SC_CHUNKS)
            def _(c):
                sl = pl.ds(c * SC_LANES, SC_LANES)
                acc = w[0] * buf[0, sl]
                for r in range(1, SC_ROWS):
                    acc = acc + w[r] * buf[r, sl]
                y_v[tl, sl] = y_v[tl, sl] + acc

        @pl.loop(0, tpw // SC_TB)
        def _(blk):
            t0 = wid * tpw + blk * SC_TB
            pltpu.sync_copy(idx_hbm.at[pl.ds(t0, SC_TB)], idx_v)
            pltpu.sync_copy(a_hbm.at[pl.ds(t0, SC_TB)], a_v)

            @pl.loop(0, SC_TB)
            def _(r):
                @pl.loop(0, SC_CHUNKS)
                def _(c):
                    y_v[r, pl.ds(c * SC_LANES, SC_LANES)] = jnp.zeros((SC_LANES,), F32)

            _sc_pipeline(n_items, gather, compute)
            pltpu.sync_copy(y_v, y_hbm.at[pl.ds(t0, SC_TB)])

    return k(act, idx, v_tab)


def _act_kernel(p_ref, g_ref, o_ref):
    p = p_ref[...]
    o_ref[...] = (p * (lax.erf(p * (1.0 / math.sqrt(2.0))) + 1.0) * 0.5) * g_ref[...]


def _expert_act(p, gates, tm=2048):
    t, n = p.shape
    spec = pl.BlockSpec((tm, n), lambda i: (i, 0))
    return pl.pallas_call(
        _act_kernel, grid=(t // tm,), in_specs=[spec, spec], out_specs=spec,
        out_shape=jax.ShapeDtypeStruct((t, n), F32),
        compiler_params=_cparams("parallel"), name="expert_act",
    )(p, gates)


def _res_kernel(x_ref, y_ref, g_ref, o_ref):
    o_ref[...] = x_ref[...] + g_ref[0] * y_ref[...]


def _res_norm_kernel(x_ref, y_ref, g_ref, fg_ref, o_ref):
    x = x_ref[...] + g_ref[0] * y_ref[...]
    ms = jnp.mean(x * x, axis=-1, keepdims=True)
    o_ref[...] = x * lax.rsqrt(ms + EPS) * fg_ref[...]


def _residual(x2, y2, g, seq, final_g=None, tm=512):
    t, d = x2.shape
    tile = pl.BlockSpec((tm, d), lambda i: (i, 0))
    gspec = pl.BlockSpec((1, 1, d), lambda i: ((i * tm) // seq, 0, 0))
    if final_g is None:
        kern, ins, args = _res_kernel, [tile, tile, gspec], (x2, y2, g)
    else:
        kern = _res_norm_kernel
        ins = [tile, tile, gspec, pl.BlockSpec((1, d), lambda i: (0, 0))]
        args = (x2, y2, g, final_g.reshape(1, d))
    return pl.pallas_call(
        kern, grid=(t // tm,), in_specs=ins, out_specs=tile,
        out_shape=jax.ShapeDtypeStruct((t, d), F32),
        compiler_params=_cparams("parallel"), name="residual",
    )(*args)


def _peer(x2, seq, gn, sc, sh, w_q, keys, u_tab, v_tab):
    keys_bf16 = keys.reshape(2 * PEER_HEADS, N_KEYS, PEER_DQH).astype(BF16)
    h2, experts, gates = _peer_route(x2, seq, gn, sc, sh, w_q.astype(BF16), keys_bf16)
    p = _peer_pre(h2, experts, u_tab)
    act = _expert_act(p, gates)
    return _peer_post(act, experts, v_tab)


def kernel(x, c, ada_w, ada_b, norm_mix_g, norm_ffn_g, cv_w_in, cv_b_in, cv_w_dw, cv_b_dw, cv_ln_g, cv_ln_b, cv_w_out, cv_b_out, sb_w_qkv, sb_w_o, pk_w_q, pk_keys, pk_u, pk_v, final_g):
    bsz, seq, d = x.shape
    t = bsz * seq
    mod = _ada_mod(c, ada_w, ada_b)
    x2 = x.reshape(t, d)
    for i in range(DEPTH):
        sh1, sc1, g1, sh2, sc2, g2 = (
            mod[i, :, n * d:(n + 1) * d].reshape(bsz, 1, d) for n in range(ADA_CHUNKS))
        j = i // 2
        if i % 2 == 0:
            u2 = _norm_glu(x2, seq, norm_mix_g[i], sc1, sh1, cv_w_in[j].astype(BF16), cv_b_in[j])
            x2 = _conv_block(u2.reshape(bsz, seq, d), x2.reshape(bsz, seq, d), g1,
                             cv_w_dw[j], cv_b_dw[j], cv_ln_g[j], cv_ln_b[j],
                             cv_w_out[j].astype(BF16), cv_b_out[j]).reshape(t, d)
        else:
            qkv = _norm_mm(x2, seq, norm_mix_g[i], sc1, sh1, sb_w_qkv[j].astype(BF16))
            o = _attention(qkv.reshape(bsz, seq, 3 * d))
            x2 = _mm_res(o.reshape(t, d), sb_w_o[j].astype(BF16), x2, g1, seq)
        y2 = _peer(x2, seq, norm_ffn_g[i], sc2, sh2, pk_w_q[i], pk_keys[i], pk_u[i], pk_v[i])
        x2 = _residual(x2, y2, g2, seq, final_g if i == DEPTH - 1 else None)
    return x2.reshape(bsz, seq, d)
```

```python
import functools
import math

import jax
import jax.numpy as jnp
from jax import lax
from jax.experimental import pallas as pl
from jax.experimental.pallas import tpu as pltpu
from jax.experimental.pallas import tpu_sc as plsc

D_MODEL = 1024
DEPTH = 2
CONV_WIDTH = 31
N_HEADS = 16
HEAD_DIM = 64
PEER_HEADS = 8
PEER_DQH = 128
N_KEYS = 128
PEER_TOPK = 16
PEER_HK = PEER_HEADS * PEER_TOPK
ADA_CHUNKS = 6
EPS = 1e-6

LANES = 128
SUBLANES = 8
SC_LANES = 16
VMEM_LIMIT = 48 * 1024 * 1024

F32 = jnp.float32
BF16 = jnp.bfloat16
NEG_INF = float("-inf")


def _cparams(*sem):
    return pltpu.CompilerParams(dimension_semantics=sem, vmem_limit_bytes=VMEM_LIMIT)


def _norm_mod(x, gn, sc, sh):
    ms = jnp.mean(x * x, axis=-1, keepdims=True)
    return (x * lax.rsqrt(ms + EPS)) * gn * (1.0 + sc) + sh


def _ada_kernel(c_ref, w_ref, b_ref, o_ref):
    c = c_ref[...]
    ca = c * jax.nn.sigmoid(c)
    o_ref[0] = jnp.dot(ca.astype(BF16), w_ref[0].astype(BF16),
                       preferred_element_type=F32) + b_ref[0]


def _ada_mod(c, ada_w, ada_b):
    depth, d, n = ada_w.shape
    bsz = c.shape[0]
    tn = 512
    return pl.pallas_call(
        _ada_kernel,
        grid=(depth, n // tn),
        in_specs=[pl.BlockSpec((bsz, d), lambda i, j: (0, 0)),
                  pl.BlockSpec((1, d, tn), lambda i, j: (i, 0, j)),
                  pl.BlockSpec((1, 1, tn), lambda i, j: (i, 0, j))],
        out_specs=pl.BlockSpec((1, bsz, tn), lambda i, j: (i, 0, j)),
        out_shape=jax.ShapeDtypeStruct((depth, bsz, n), F32),
        compiler_params=_cparams("parallel", "parallel"),
        name="ada_mod",
    )(c, ada_w, ada_b.reshape(depth, 1, n))


def _norm_mm_kernel(x_ref, gn_ref, sc_ref, sh_ref, w_ref, o_ref):
    h = _norm_mod(x_ref[...], gn_ref[...], sc_ref[0], sh_ref[0])
    o_ref[...] = jnp.dot(h.astype(BF16), w_ref[...],
                         preferred_element_type=F32).astype(o_ref.dtype)


def _norm_mm(x2, seq, gn, sc, sh, w_bf16, out_dtype, tm=256):
    t, d = x2.shape
    n = w_bf16.shape[1]
    per_seq = lambda i: ((i * tm) // seq, 0, 0)
    return pl.pallas_call(
        _norm_mm_kernel,
        grid=(t // tm,),
        in_specs=[pl.BlockSpec((tm, d), lambda i: (i, 0)),
                  pl.BlockSpec((1, d), lambda i: (0, 0)),
                  pl.BlockSpec((1, 1, d), per_seq),
                  pl.BlockSpec((1, 1, d), per_seq),
                  pl.BlockSpec((d, n), lambda i: (0, 0))],
        out_specs=pl.BlockSpec((tm, n), lambda i: (i, 0)),
        out_shape=jax.ShapeDtypeStruct((t, n), out_dtype),
        compiler_params=_cparams("parallel"),
        name="norm_mm",
    )(x2, gn.reshape(1, d), sc, sh, w_bf16)


def _norm_glu_kernel(x_ref, gn_ref, sc_ref, sh_ref, wa_ref, wg_ref, ba_ref, bg_ref, o_ref):
    h = _norm_mod(x_ref[...], gn_ref[...], sc_ref[0], sh_ref[0]).astype(BF16)
    a = jnp.dot(h, wa_ref[...], preferred_element_type=F32) + ba_ref[...]
    g = jnp.dot(h, wg_ref[...], preferred_element_type=F32) + bg_ref[...]
    o_ref[...] = a * jax.nn.sigmoid(g)


def _norm_glu(x2, seq, gn, sc, sh, w_in_bf16, b_in, tm=256):
    t, d = x2.shape
    n = w_in_bf16.shape[1] // 2
    per_seq = lambda i: ((i * tm) // seq, 0, 0)
    b2 = b_in.reshape(1, 2 * n)
    return pl.pallas_call(
        _norm_glu_kernel,
        grid=(t // tm,),
        in_specs=[pl.BlockSpec((tm, d), lambda i: (i, 0)),
                  pl.BlockSpec((1, d), lambda i: (0, 0)),
                  pl.BlockSpec((1, 1, d), per_seq),
                  pl.BlockSpec((1, 1, d), per_seq),
                  pl.BlockSpec((d, n), lambda i: (0, 0)),
                  pl.BlockSpec((d, n), lambda i: (0, 1)),
                  pl.BlockSpec((1, n), lambda i: (0, 0)),
                  pl.BlockSpec((1, n), lambda i: (0, 1))],
        out_specs=pl.BlockSpec((tm, n), lambda i: (i, 0)),
        out_shape=jax.ShapeDtypeStruct((t, n), F32),
        compiler_params=_cparams("parallel"),
        name="norm_glu",
    )(x2, gn.reshape(1, d), sc, sh, w_in_bf16, w_in_bf16, b2, b2)


HALO = 32


def _conv_kernel(cur_ref, prev_ref, wdw_ref, bdw_ref, lng_ref, lnb_ref, wout_ref,
                 bout_ref, x_ref, g_ref, o_ref, win_s):
    ts = cur_ref.shape[1]
    first = pl.program_id(1) == 0
    win_s[0:HALO, :] = jnp.where(first, 0.0, prev_ref[0])
    win_s[HALO:HALO + ts, :] = cur_ref[0]
    acc = jnp.zeros(cur_ref.shape[1:], F32) + bdw_ref[...]
    off = HALO - (CONV_WIDTH - 1)
    for k in range(CONV_WIDTH):
        acc = acc + win_s[off + k:off + k + ts, :] * wdw_ref[k:k + 1, :]
    mu = jnp.mean(acc, axis=-1, keepdims=True)
    cen = acc - mu
    var = jnp.mean(cen * cen, axis=-1, keepdims=True)
    y = cen * lax.rsqrt(var + EPS) * lng_ref[...] + lnb_ref[...]
    y = y * jax.nn.sigmoid(y)
    m = jnp.dot(y.astype(BF16), wout_ref[...], preferred_element_type=F32) + bout_ref[...]
    o_ref[0] = x_ref[0] + g_ref[0] * m


def _conv_block(u3, x3, g1, w_dw, b_dw, ln_g, ln_b, w_out_bf16, b_out, ts=256):
    bsz, seq, d = u3.shape
    r = ts // HALO
    wdw = jnp.concatenate([w_dw, jnp.zeros((HALO - CONV_WIDTH, d), F32)], axis=0)
    vec = lambda b, i: (0, 0)
    return pl.pallas_call(
        _conv_kernel,
        grid=(bsz, seq // ts),
        in_specs=[pl.BlockSpec((1, ts, d), lambda b, i: (b, i, 0)),
                  pl.BlockSpec((1, HALO, d), lambda b, i: (b, jnp.maximum(i * r - 1, 0), 0)),
                  pl.BlockSpec((HALO, d), vec),
                  pl.BlockSpec((1, d), vec),
                  pl.BlockSpec((1, d), vec),
                  pl.BlockSpec((1, d), vec),
                  pl.BlockSpec((d, d), vec),
                  pl.BlockSpec((1, d), vec),
                  pl.BlockSpec((1, ts, d), lambda b, i: (b, i, 0)),
                  pl.BlockSpec((1, 1, d), lambda b, i: (b, 0, 0))],
        out_specs=pl.BlockSpec((1, ts, d), lambda b, i: (b, i, 0)),
        out_shape=jax.ShapeDtypeStruct((bsz, seq, d), F32),
        scratch_shapes=[pltpu.VMEM((HALO + ts, d), F32)],
        compiler_params=_cparams("parallel", "arbitrary"),
        name="conv_block",
    )(u3, u3, wdw, b_dw.reshape(1, d), ln_g.reshape(1, d), ln_b.reshape(1, d),
      w_out_bf16, b_out.reshape(1, d), x3, g1)


ATT_T = 256


def _attn_kernel(q_ref, k_ref, v_ref, o_ref):
    t = ATT_T
    qi = pl.program_id(2)
    q2 = q_ref[0] * (1.0 / math.sqrt(HEAD_DIM))
    lane = lax.broadcasted_iota(jnp.int32, (1, LANES), 1)
    head_mask = (lane < HEAD_DIM, lane >= HEAD_DIM)
    row = lax.broadcasted_iota(jnp.int32, (t, t), 0)
    col = lax.broadcasted_iota(jnp.int32, (t, t), 1)
    tri = (row >= col).astype(BF16)
    tri2 = jnp.concatenate([tri, tri], axis=0)
    qh = tuple(jnp.where(m, q2, jnp.zeros_like(q2)) for m in head_mask)
    nt = (((1,), (1,)), ((), ()))

    def tile(kt, suf, acc, diag):
        start = pl.multiple_of(kt * t, t)
        k2 = k_ref[0, pl.ds(start, t), :]
        v2 = v_ref[0, pl.ds(start, t), :]
        new_suf = []
        for h in range(2):
            z = lax.dot_general(qh[h], k2, nt, preferred_element_type=F32)
            sp = jnp.maximum(z, 0.0) + jnp.log(1.0 + jnp.exp(-jnp.abs(z)))
            if diag:
                sp = jnp.where(col < row, sp, 0.0)
            hi = sp.astype(BF16)
            mid = (sp - hi.astype(F32)).astype(BF16)
            cs = jnp.dot(jnp.concatenate([hi, mid], axis=1), tri2,
                         preferred_element_type=F32)
            a = jnp.exp(z - cs - suf[h])
            if diag:
                a = jnp.where(col < row, a, 0.0)
            pv = jnp.dot(a.astype(BF16), v2, preferred_element_type=F32)
            acc = acc + jnp.where(head_mask[h], pv, 0.0)
            new_suf.append(suf[h] + cs[:, 0:1])
        return tuple(new_suf), acc

    zero = jnp.zeros((t, 1), F32)
    suf, acc = tile(qi, (zero, zero), jnp.zeros((t, LANES), F32), True)

    def body(n, carry):
        s0, s1, acc = carry
        (s0, s1), acc = tile(qi - n, (s0, s1), acc, False)
        return s0, s1, acc

    _, _, acc = lax.fori_loop(1, qi + 1, body, (suf[0], suf[1], acc))
    o_ref[0] = acc


def _attention(qkv3):
    bsz, seq, n3 = qkv3.shape
    d = n3 // 3
    nb = d // LANES
    return pl.pallas_call(
        _attn_kernel,
        grid=(bsz, nb, seq // ATT_T),
        in_specs=[pl.BlockSpec((1, ATT_T, LANES), lambda b, h, i: (b, i, h)),
                  pl.BlockSpec((1, seq, LANES), lambda b, h, i: (b, 0, nb + h)),
                  pl.BlockSpec((1, seq, LANES), lambda b, h, i: (b, 0, 2 * nb + h))],
        out_specs=pl.BlockSpec((1, ATT_T, LANES), lambda b, h, i: (b, i, h)),
        out_shape=jax.ShapeDtypeStruct((bsz, seq, d), F32),
        compiler_params=_cparams("parallel", "parallel", "arbitrary"),
        name="sb_attention",
    )(qkv3, qkv3, qkv3)


def _mm_res_kernel(a_ref, w_ref, x_ref, g_ref, o_ref):
    m = jnp.dot(a_ref[...].astype(BF16), w_ref[...], preferred_element_type=F32)
    o_ref[...] = x_ref[...] + g_ref[0] * m


def _mm_res(a2, w_bf16, x2, g, seq, tm=512):
    t, d = x2.shape
    k = a2.shape[1]
    return pl.pallas_call(
        _mm_res_kernel,
        grid=(t // tm,),
        in_specs=[pl.BlockSpec((tm, k), lambda i: (i, 0)),
                  pl.BlockSpec((k, d), lambda i: (0, 0)),
                  pl.BlockSpec((tm, d), lambda i: (i, 0)),
                  pl.BlockSpec((1, 1, d), lambda i: ((i * tm) // seq, 0, 0))],
        out_specs=pl.BlockSpec((tm, d), lambda i: (i, 0)),
        out_shape=jax.ShapeDtypeStruct((t, d), F32),
        compiler_params=_cparams("parallel"),
        name="mm_res",
    )(a2, w_bf16, x2, g)


PAIR_ROWS = tuple(PEER_TOPK // (i + 1) for i in range(PEER_TOPK))


def _extract_top(s, payload, k):
    rows = lax.broadcasted_iota(jnp.int32, s.shape, 0)
    big = s.shape[0]
    vals, outs = [], []
    for _ in range(k):
        m = jnp.max(s, axis=0, keepdims=True)
        r = jnp.min(jnp.where(s == m, rows, big), axis=0, keepdims=True)
        sel = rows == r
        vals.append(m)
        if payload is None:
            outs.append(r)
        else:
            outs.append(jnp.max(jnp.where(sel, payload, -1), axis=0, keepdims=True))
        s = jnp.where(sel, NEG_INF, s)
    return jnp.concatenate(vals, axis=0), jnp.concatenate(outs, axis=0)


def _pair_candidates(a, ia, b, ib):
    sub = lax.broadcasted_iota(jnp.int32, (SUBLANES, a.shape[1]), 0)
    cand, ids = [], []
    for i in range(SUBLANES):
        for j0 in range(0, PAIR_ROWS[i], SUBLANES):
            n = min(PAIR_ROWS[i] - j0, SUBLANES)
            c = a[i:i + 1, :] + b[j0:j0 + SUBLANES, :]
            e = ia[i:i + 1, :] * N_KEYS + ib[j0:j0 + SUBLANES, :]
            if n < SUBLANES:
                c = jnp.where(sub < n, c, NEG_INF)
            cand.append(c)
            ids.append(e)
    cand.append(a[SUBLANES:, :] + b[0:1, :])
    ids.append(ia[SUBLANES:, :] * N_KEYS + ib[0:1, :])
    return jnp.concatenate(cand, axis=0), jnp.concatenate(ids, axis=0)


def _peer_route_kernel(row_base, x_ref, gn_ref, sc_ref, sh_ref, wq_ref, keys_ref,
                       h_ref, e_ref, g_ref, q_s, e_s, g_s):
    h = _norm_mod(x_ref[...], gn_ref[...], sc_ref[0], sh_ref[0])
    h_ref[...] = h
    q = jnp.dot(h.astype(BF16), wq_ref[...], preferred_element_type=F32)
    for j in range(2 * PEER_HEADS):
        q_s[j] = q[:, j * PEER_DQH:(j + 1) * PEER_DQH].astype(BF16)
    nt = (((1,), (1,)), ((), ()))

    def head(hd, carry):
        tops = []
        for c in range(2):
            s_t = lax.dot_general(keys_ref[2 * hd + c], q_s[2 * hd + c], nt,
                                  preferred_element_type=F32)
            tops.append(_extract_top(s_t, None, PEER_TOPK))
        cand, ids = _pair_candidates(tops[0][0], tops[0][1], tops[1][0], tops[1][1])
        best, experts = _extract_top(cand, ids, PEER_TOPK)
        ex = jnp.exp(best - best[0:1, :])
        g_s[hd] = ex / jnp.sum(ex, axis=0, keepdims=True)
        e_s[hd] = experts
        return carry

    lax.fori_loop(0, PEER_HEADS, head, 0)
    tm = x_ref.shape[0]
    e_ref[...] = e_s[...].reshape(PEER_HK, tm).T + row_base
    g_ref[...] = g_s[...].reshape(PEER_HK, tm).T


def _peer_route(x2, seq, gn, sc, sh, wq_bf16, keys_bf16, row_base, tm=256):
    t, d = x2.shape
    nq = wq_bf16.shape[1]
    per_seq = lambda i: ((i * tm) // seq, 0, 0)
    return pl.pallas_call(
        functools.partial(_peer_route_kernel, row_base),
        grid=(t // tm,),
        in_specs=[pl.BlockSpec((tm, d), lambda i: (i, 0)),
                  pl.BlockSpec((1, d), lambda i: (0, 0)),
                  pl.BlockSpec((1, 1, d), per_seq),
                  pl.BlockSpec((1, 1, d), per_seq),
                  pl.BlockSpec((d, nq), lambda i: (0, 0)),
                  pl.BlockSpec((2 * PEER_HEADS, N_KEYS, PEER_DQH), lambda i: (0, 0, 0))],
        out_specs=[pl.BlockSpec((tm, d), lambda i: (i, 0)),
                   pl.BlockSpec((tm, PEER_HK), lambda i: (i, 0)),
                   pl.BlockSpec((tm, PEER_HK), lambda i: (i, 0))],
        out_shape=[jax.ShapeDtypeStruct((t, d), F32),
                   jax.ShapeDtypeStruct((t, PEER_HK), jnp.int32),
                   jax.ShapeDtypeStruct((t, PEER_HK), F32)],
        scratch_shapes=[pltpu.VMEM((2 * PEER_HEADS, tm, PEER_DQH), BF16),
                        pltpu.VMEM((PEER_HEADS, PEER_TOPK, tm), jnp.int32),
                        pltpu.VMEM((PEER_HEADS, PEER_TOPK, tm), F32)],
        compiler_params=_cparams("parallel"),
        name="peer_route",
    )(x2, gn.reshape(1, d), sc, sh, wq_bf16, keys_bf16)


SC_ROWS = 16
SC_GROUPS = PEER_HK // SC_ROWS
SC_TB = 16
SC_CHUNKS = D_MODEL // SC_LANES
SC_NBUF = 4


def _sc_mesh():
    return plsc.VectorSubcoreMesh(core_axis_name="c", subcore_axis_name="s")


def _sc_pipeline(n_items, gather, compute):
    ahead = SC_NBUF - 1
    for i in range(ahead):
        gather(i, i).start()

    @pl.loop(0, n_items, step=SC_NBUF)
    def _(it):
        for b in range(SC_NBUF):
            i = it + b
            gather(i, b).wait()

            @pl.when(i + ahead < n_items)
            def _():
                gather(i + ahead, (b + ahead) % SC_NBUF).start()

            compute(i, b)


def _tree_sum(terms):
    while len(terms) > 1:
        terms = [terms[i] + terms[i + 1] for i in range(0, len(terms), 2)]
    return terms[0]


def _peer_pre(h2, idx, u_tab):
    t, d = h2.shape
    info = plsc.get_sparse_core_info()
    nc, nw = info.num_cores, info.num_cores * info.num_subcores
    tpw = t // nw
    n_items = SC_TB * SC_GROUPS

    @functools.partial(
        pl.kernel, mesh=_sc_mesh(),
        out_type=jax.ShapeDtypeStruct((t, PEER_HK), F32),
        scratch_types=[pltpu.VMEM((SC_TB, PEER_HK), jnp.int32),
                       pltpu.VMEM((SC_TB, d), F32),
                       pltpu.VMEM((SC_TB, PEER_HK), F32),
                       pltpu.VMEM((SC_ROWS, SC_LANES), F32)]
                      + [pltpu.VMEM((SC_ROWS, d), F32)] * SC_NBUF
                      + [pltpu.SemaphoreType.DMA] * SC_NBUF,
        compiler_params=pltpu.CompilerParams(needs_layout_passes=False),
        name="peer_pre")
    def k(h_hbm, idx_hbm, u_hbm, p_hbm, idx_v, h_v, p_v, tr_v, *ring):
        wid = lax.axis_index("s") * nc + lax.axis_index("c")
        bufs, sems = ring[:SC_NBUF], ring[SC_NBUF:]
        lane = lax.iota(jnp.int32, SC_LANES)

        def gather(it, b):
            tl, g = it // SC_GROUPS, it % SC_GROUPS
            return pltpu.make_async_copy(
                u_hbm.at[idx_v.at[tl, pl.ds(g * SC_ROWS, SC_ROWS)]], bufs[b], sems[b])

        def compute(it, b):
            tl, g = it // SC_GROUPS, it % SC_GROUPS
            buf = bufs[b]

            def body(c, accs):
                xc = h_v[tl, pl.ds(c * SC_LANES, SC_LANES)]
                return tuple(accs[r] + buf[r, pl.ds(c * SC_LANES, SC_LANES)] * xc
                             for r in range(SC_ROWS))

            zero = jnp.zeros((SC_LANES,), F32)
            accs = lax.fori_loop(0, SC_CHUNKS, body, (zero,) * SC_ROWS)
            for r in range(SC_ROWS):
                tr_v[r, :] = accs[r]
            cols = [plsc.load_gather(tr_v, [lane, jnp.full((SC_LANES,), j, jnp.int32)])
                    for j in range(SC_LANES)]
            p_v[tl, pl.ds(g * SC_ROWS, SC_ROWS)] = _tree_sum(cols)

        @pl.loop(0, tpw // SC_TB)
        def _(blk):
            t0 = wid * tpw + blk * SC_TB
            pltpu.sync_copy(idx_hbm.at[pl.ds(t0, SC_TB)], idx_v)
            pltpu.sync_copy(h_hbm.at[pl.ds(t0, SC_TB)], h_v)
            _sc_pipeline(n_items, gather, compute)
            pltpu.sync_copy(p_v, p_hbm.at[pl.ds(t0, SC_TB)])

    return k(h2, idx, u_tab)


def _peer_post(act, idx, v_tab):
    t = act.shape[0]
    d = v_tab.shape[1]
    info = plsc.get_sparse_core_info()
    nc, nw = info.num_cores, info.num_cores * info.num_subcores
    tpw = t // nw
    n_items = SC_TB * SC_GROUPS

    @functools.partial(
        pl.kernel, mesh=_sc_mesh(),
        out_type=jax.ShapeDtypeStruct((t, d), F32),
        scratch_types=[pltpu.VMEM((SC_TB, PEER_HK), jnp.int32),
                       pltpu.VMEM((SC_TB, PEER_HK), F32),
                       pltpu.VMEM((SC_TB, d), F32)]
                      + [pltpu.VMEM((SC_ROWS, d), F32)] * SC_NBUF
                      + [pltpu.SemaphoreType.DMA] * SC_NBUF,
        compiler_params=pltpu.CompilerParams(needs_layout_passes=False),
        name="peer_post")
    def k(a_hbm, idx_hbm, v_hbm, y_hbm, idx_v, a_v, y_v, *ring):
        wid = lax.axis_index("s") * nc + lax.axis_index("c")
        bufs, sems = ring[:SC_NBUF], ring[SC_NBUF:]

        def gather(it, b):
            tl, g = it // SC_GROUPS, it % SC_GROUPS
            return pltpu.make_async_copy(
                v_hbm.at[idx_v.at[tl, pl.ds(g * SC_ROWS, SC_ROWS)]], bufs[b], sems[b])

        def compute(it, b):
            tl, g = it // SC_GROUPS, it % SC_GROUPS
            buf = bufs[b]
            tlv = jnp.full((SC_LANES,), tl, jnp.int32)
            w = tuple(plsc.load_gather(a_v, [tlv, jnp.full((SC_LANES,), g * SC_ROWS + r, jnp.int32)])
                      for r in range(SC_ROWS))

            @plsc.parallel_loop(0, SC_CHUNKS, unroll=2)
            def _(c):
                sl = pl.ds(c * SC_LANES, SC_LANES)
                plsc.addupdate(y_v.at[tl, sl],
                               _tree_sum([w[r] * buf[r, sl] for r in range(SC_ROWS)]))

        @pl.loop(0, tpw // SC_TB)
        def _(blk):
            t0 = wid * tpw + blk * SC_TB
            pltpu.sync_copy(idx_hbm.at[pl.ds(t0, SC_TB)], idx_v)
            pltpu.sync_copy(a_hbm.at[pl.ds(t0, SC_TB)], a_v)

            @pl.loop(0, SC_TB)
            def _(r):
                @plsc.parallel_loop(0, SC_CHUNKS, unroll=4)
                def _(c):
                    y_v[r, pl.ds(c * SC_LANES, SC_LANES)] = jnp.zeros((SC_LANES,), F32)

            _sc_pipeline(n_items, gather, compute)
            pltpu.sync_copy(y_v, y_hbm.at[pl.ds(t0, SC_TB)])

    return k(act, idx, v_tab)


def _act_kernel(p_ref, g_ref, o_ref):
    p = p_ref[...]
    o_ref[...] = (p * (lax.erf(p * (1.0 / math.sqrt(2.0))) + 1.0) * 0.5) * g_ref[...]


def _expert_act(p, gates, tm=2048):
    t, n = p.shape
    spec = pl.BlockSpec((tm, n), lambda i: (i, 0))
    return pl.pallas_call(
        _act_kernel, grid=(t // tm,), in_specs=[spec, spec], out_specs=spec,
        out_shape=jax.ShapeDtypeStruct((t, n), F32),
        compiler_params=_cparams("parallel"), name="expert_act",
    )(p, gates)


def _res_kernel(x_ref, y_ref, g_ref, o_ref):
    o_ref[...] = x_ref[...] + g_ref[0] * y_ref[...]


def _res_norm_kernel(x_ref, y_ref, g_ref, fg_ref, o_ref):
    x = x_ref[...] + g_ref[0] * y_ref[...]
    ms = jnp.mean(x * x, axis=-1, keepdims=True)
    o_ref[...] = x * lax.rsqrt(ms + EPS) * fg_ref[...]


def _residual(x2, y2, g, seq, final_g=None, tm=512):
    t, d = x2.shape
    tile = pl.BlockSpec((tm, d), lambda i: (i, 0))
    gspec = pl.BlockSpec((1, 1, d), lambda i: ((i * tm) // seq, 0, 0))
    if final_g is None:
        kern, ins, args = _res_kernel, [tile, tile, gspec], (x2, y2, g)
    else:
        kern = _res_norm_kernel
        ins = [tile, tile, gspec, pl.BlockSpec((1, d), lambda i: (0, 0))]
        args = (x2, y2, g, final_g.reshape(1, d))
    return pl.pallas_call(
        kern, grid=(t // tm,), in_specs=ins, out_specs=tile,
        out_shape=jax.ShapeDtypeStruct((t, d), F32),
        compiler_params=_cparams("parallel"), name="residual",
    )(*args)


def _peer(x2, seq, gn, sc, sh, wq_bf16, keys_bf16, u_all, v_all, layer):
    rows = _peer_route(x2, seq, gn, sc, sh, wq_bf16, keys_bf16, layer * N_KEYS * N_KEYS)
    h2, experts, gates = rows
    p = _peer_pre(h2, experts, u_all)
    act = _expert_act(p, gates)
    return _peer_post(act, experts, v_all)


BATCH_SPLIT = 2


def kernel(x, c, ada_w, ada_b, norm_mix_g, norm_ffn_g, cv_w_in, cv_b_in, cv_w_dw, cv_b_dw, cv_ln_g, cv_ln_b, cv_w_out, cv_b_out, sb_w_qkv, sb_w_o, pk_w_q, pk_keys, pk_u, pk_v, final_g):
    bsz, seq, d = x.shape
    mod = _ada_mod(c, ada_w, ada_b)
    w_in, w_out = cv_w_in.astype(BF16), cv_w_out.astype(BF16)
    w_qkv, w_o = sb_w_qkv.astype(BF16), sb_w_o.astype(BF16)
    w_q = pk_w_q.astype(BF16)
    keys = pk_keys.reshape(DEPTH, 2 * PEER_HEADS, N_KEYS, PEER_DQH).astype(BF16)
    u_all = pk_u.reshape(DEPTH * N_KEYS * N_KEYS, d)
    v_all = pk_v.reshape(DEPTH * N_KEYS * N_KEYS, d)
    nb = bsz // BATCH_SPLIT
    t = nb * seq
    outs = []
    for part in range(BATCH_SPLIT):
        lo = part * nb
        x2 = x[lo:lo + nb].reshape(t, d)
        for i in range(DEPTH):
            sh1, sc1, g1, sh2, sc2, g2 = (
                mod[i, lo:lo + nb, n * d:(n + 1) * d].reshape(nb, 1, d) for n in range(ADA_CHUNKS))
            j = i // 2
            if i % 2 == 0:
                u2 = _norm_glu(x2, seq, norm_mix_g[i], sc1, sh1, w_in[j], cv_b_in[j])
                x2 = _conv_block(u2.reshape(nb, seq, d), x2.reshape(nb, seq, d), g1,
                                 cv_w_dw[j], cv_b_dw[j], cv_ln_g[j], cv_ln_b[j],
                                 w_out[j], cv_b_out[j]).reshape(t, d)
            else:
                qkv = _norm_mm(x2, seq, norm_mix_g[i], sc1, sh1, w_qkv[j], BF16)
                o = _attention(qkv.reshape(nb, seq, 3 * d))
                x2 = _mm_res(o.reshape(t, d), w_o[j], x2, g1, seq)
            y2 = _peer(x2, seq, norm_ffn_g[i], sc2, sh2, w_q[i], keys[i], u_all, v_all, i)
            x2 = _residual(x2, y2, g2, seq, final_g if i == DEPTH - 1 else None)
        outs.append(x2.reshape(nb, seq, d))
    return jnp.concatenate(outs, axis=0)
```

```python
import functools
import math

import jax
import jax.numpy as jnp
from jax import lax
from jax.experimental import pallas as pl
from jax.experimental.pallas import tpu as pltpu
from jax.experimental.pallas import tpu_sc as plsc

D_MODEL = 1024
DEPTH = 2
CONV_WIDTH = 31
N_HEADS = 16
HEAD_DIM = 64
PEER_HEADS = 8
PEER_DQH = 128
N_KEYS = 128
PEER_TOPK = 16
PEER_HK = PEER_HEADS * PEER_TOPK
ADA_CHUNKS = 6
EPS = 1e-6

LANES = 128
SUBLANES = 8
SC_LANES = 16
VMEM_LIMIT = 48 * 1024 * 1024

F32 = jnp.float32
BF16 = jnp.bfloat16
NEG_INF = float("-inf")


def _cparams(*sem):
    return pltpu.CompilerParams(dimension_semantics=sem, vmem_limit_bytes=VMEM_LIMIT)


def _norm_mod(x, gn, sc, sh):
    ms = jnp.mean(x * x, axis=-1, keepdims=True)
    return (x * lax.rsqrt(ms + EPS)) * gn * (1.0 + sc) + sh


def _ada_kernel(c_ref, w_ref, b_ref, o_ref):
    c = c_ref[...]
    ca = c * jax.nn.sigmoid(c)
    o_ref[0] = jnp.dot(ca.astype(BF16), w_ref[0].astype(BF16),
                       preferred_element_type=F32) + b_ref[0]


def _ada_mod(c, ada_w, ada_b):
    depth, d, n = ada_w.shape
    bsz = c.shape[0]
    tn = 512
    return pl.pallas_call(
        _ada_kernel,
        grid=(depth, n // tn),
        in_specs=[pl.BlockSpec((bsz, d), lambda i, j: (0, 0)),
                  pl.BlockSpec((1, d, tn), lambda i, j: (i, 0, j)),
                  pl.BlockSpec((1, 1, tn), lambda i, j: (i, 0, j))],
        out_specs=pl.BlockSpec((1, bsz, tn), lambda i, j: (i, 0, j)),
        out_shape=jax.ShapeDtypeStruct((depth, bsz, n), F32),
        compiler_params=_cparams("parallel", "parallel"),
        name="ada_mod",
    )(c, ada_w, ada_b.reshape(depth, 1, n))


def _norm_mm_kernel(x_ref, gn_ref, sc_ref, sh_ref, w_ref, o_ref):
    h = _norm_mod(x_ref[...], gn_ref[...], sc_ref[0], sh_ref[0])
    o_ref[...] = jnp.dot(h.astype(BF16), w_ref[...],
                         preferred_element_type=F32).astype(o_ref.dtype)


def _norm_mm(x2, seq, gn, sc, sh, w_bf16, out_dtype, tm=256):
    t, d = x2.shape
    n = w_bf16.shape[1]
    per_seq = lambda i: ((i * tm) // seq, 0, 0)
    return pl.pallas_call(
        _norm_mm_kernel,
        grid=(t // tm,),
        in_specs=[pl.BlockSpec((tm, d), lambda i: (i, 0)),
                  pl.BlockSpec((1, d), lambda i: (0, 0)),
                  pl.BlockSpec((1, 1, d), per_seq),
                  pl.BlockSpec((1, 1, d), per_seq),
                  pl.BlockSpec((d, n), lambda i: (0, 0))],
        out_specs=pl.BlockSpec((tm, n), lambda i: (i, 0)),
        out_shape=jax.ShapeDtypeStruct((t, n), out_dtype),
        compiler_params=_cparams("parallel"),
        name="norm_mm",
    )(x2, gn.reshape(1, d), sc, sh, w_bf16)


def _norm_glu_kernel(x_ref, gn_ref, sc_ref, sh_ref, wa_ref, wg_ref, ba_ref, bg_ref, o_ref):
    h = _norm_mod(x_ref[...], gn_ref[...], sc_ref[0], sh_ref[0]).astype(BF16)
    a = jnp.dot(h, wa_ref[...], preferred_element_type=F32) + ba_ref[...]
    g = jnp.dot(h, wg_ref[...], preferred_element_type=F32) + bg_ref[...]
    o_ref[...] = a * jax.nn.sigmoid(g)


def _norm_glu(x2, seq, gn, sc, sh, w_in_bf16, b_in, tm=256):
    t, d = x2.shape
    n = w_in_bf16.shape[1] // 2
    per_seq = lambda i: ((i * tm) // seq, 0, 0)
    b2 = b_in.reshape(1, 2 * n)
    return pl.pallas_call(
        _norm_glu_kernel,
        grid=(t // tm,),
        in_specs=[pl.BlockSpec((tm, d), lambda i: (i, 0)),
                  pl.BlockSpec((1, d), lambda i: (0, 0)),
                  pl.BlockSpec((1, 1, d), per_seq),
                  pl.BlockSpec((1, 1, d), per_seq),
                  pl.BlockSpec((d, n), lambda i: (0, 0)),
                  pl.BlockSpec((d, n), lambda i: (0, 1)),
                  pl.BlockSpec((1, n), lambda i: (0, 0)),
                  pl.BlockSpec((1, n), lambda i: (0, 1))],
        out_specs=pl.BlockSpec((tm, n), lambda i: (i, 0)),
        out_shape=jax.ShapeDtypeStruct((t, n), F32),
        compiler_params=_cparams("parallel"),
        name="norm_glu",
    )(x2, gn.reshape(1, d), sc, sh, w_in_bf16, w_in_bf16, b2, b2)


HALO = 32


def _conv_kernel(cur_ref, prev_ref, wdw_ref, bdw_ref, lng_ref, lnb_ref, wout_ref,
                 bout_ref, x_ref, g_ref, o_ref, win_s):
    ts = cur_ref.shape[1]
    first = pl.program_id(1) == 0
    win_s[0:HALO, :] = jnp.where(first, 0.0, prev_ref[0])
    win_s[HALO:HALO + ts, :] = cur_ref[0]
    acc = jnp.zeros(cur_ref.shape[1:], F32) + bdw_ref[...]
    off = HALO - (CONV_WIDTH - 1)
    for k in range(CONV_WIDTH):
        acc = acc + win_s[off + k:off + k + ts, :] * wdw_ref[k:k + 1, :]
    mu = jnp.mean(acc, axis=-1, keepdims=True)
    cen = acc - mu
    var = jnp.mean(cen * cen, axis=-1, keepdims=True)
    y = cen * lax.rsqrt(var + EPS) * lng_ref[...] + lnb_ref[...]
    y = y * jax.nn.sigmoid(y)
    m = jnp.dot(y.astype(BF16), wout_ref[...], preferred_element_type=F32) + bout_ref[...]
    o_ref[0] = x_ref[0] + g_ref[0] * m


def _conv_block(u3, x3, g1, w_dw, b_dw, ln_g, ln_b, w_out_bf16, b_out, ts=256):
    bsz, seq, d = u3.shape
    r = ts // HALO
    wdw = jnp.concatenate([w_dw, jnp.zeros((HALO - CONV_WIDTH, d), F32)], axis=0)
    vec = lambda b, i: (0, 0)
    return pl.pallas_call(
        _conv_kernel,
        grid=(bsz, seq // ts),
        in_specs=[pl.BlockSpec((1, ts, d), lambda b, i: (b, i, 0)),
                  pl.BlockSpec((1, HALO, d), lambda b, i: (b, jnp.maximum(i * r - 1, 0), 0)),
                  pl.BlockSpec((HALO, d), vec),
                  pl.BlockSpec((1, d), vec),
                  pl.BlockSpec((1, d), vec),
                  pl.BlockSpec((1, d), vec),
                  pl.BlockSpec((d, d), vec),
                  pl.BlockSpec((1, d), vec),
                  pl.BlockSpec((1, ts, d), lambda b, i: (b, i, 0)),
                  pl.BlockSpec((1, 1, d), lambda b, i: (b, 0, 0))],
        out_specs=pl.BlockSpec((1, ts, d), lambda b, i: (b, i, 0)),
        out_shape=jax.ShapeDtypeStruct((bsz, seq, d), F32),
        scratch_shapes=[pltpu.VMEM((HALO + ts, d), F32)],
        compiler_params=_cparams("parallel", "arbitrary"),
        name="conv_block",
    )(u3, u3, wdw, b_dw.reshape(1, d), ln_g.reshape(1, d), ln_b.reshape(1, d),
      w_out_bf16, b_out.reshape(1, d), x3, g1)


ATT_T = 256


def _attn_kernel(q_ref, k_ref, v_ref, o_ref):
    t = ATT_T
    qi = pl.program_id(2)
    q2 = q_ref[0] * (1.0 / math.sqrt(HEAD_DIM))
    lane = lax.broadcasted_iota(jnp.int32, (1, LANES), 1)
    head_mask = (lane < HEAD_DIM, lane >= HEAD_DIM)
    row = lax.broadcasted_iota(jnp.int32, (t, t), 0)
    col = lax.broadcasted_iota(jnp.int32, (t, t), 1)
    tri = (row >= col).astype(BF16)
    tri2 = jnp.concatenate([tri, tri], axis=0)
    qh = tuple(jnp.where(m, q2, jnp.zeros_like(q2)) for m in head_mask)
    nt = (((1,), (1,)), ((), ()))

    def tile(kt, suf, acc, diag):
        start = pl.multiple_of(kt * t, t)
        k2 = k_ref[0, pl.ds(start, t), :]
        v2 = v_ref[0, pl.ds(start, t), :]
        new_suf = []
        for h in range(2):
            z = lax.dot_general(qh[h], k2, nt, preferred_element_type=F32)
            sp = jnp.maximum(z, 0.0) + jnp.log(1.0 + jnp.exp(-jnp.abs(z)))
            if diag:
                sp = jnp.where(col < row, sp, 0.0)
            hi = sp.astype(BF16)
            mid = (sp - hi.astype(F32)).astype(BF16)
            cs = jnp.dot(jnp.concatenate([hi, mid], axis=1), tri2,
                         preferred_element_type=F32)
            a = jnp.exp(z - cs - suf[h])
            if diag:
                a = jnp.where(col < row, a, 0.0)
            pv = jnp.dot(a.astype(BF16), v2, preferred_element_type=F32)
            acc = acc + jnp.where(head_mask[h], pv, 0.0)
            new_suf.append(suf[h] + cs[:, 0:1])
        return tuple(new_suf), acc

    zero = jnp.zeros((t, 1), F32)
    suf, acc = tile(qi, (zero, zero), jnp.zeros((t, LANES), F32), True)

    def body(n, carry):
        s0, s1, acc = carry
        (s0, s1), acc = tile(qi - n, (s0, s1), acc, False)
        return s0, s1, acc

    _, _, acc = lax.fori_loop(1, qi + 1, body, (suf[0], suf[1], acc))
    o_ref[0] = acc


def _attention(qkv3):
    bsz, seq, n3 = qkv3.shape
    d = n3 // 3
    nb = d // LANES
    return pl.pallas_call(
        _attn_kernel,
        grid=(bsz, nb, seq // ATT_T),
        in_specs=[pl.BlockSpec((1, ATT_T, LANES), lambda b, h, i: (b, i, h)),
                  pl.BlockSpec((1, seq, LANES), lambda b, h, i: (b, 0, nb + h)),
                  pl.BlockSpec((1, seq, LANES), lambda b, h, i: (b, 0, 2 * nb + h))],
        out_specs=pl.BlockSpec((1, ATT_T, LANES), lambda b, h, i: (b, i, h)),
        out_shape=jax.ShapeDtypeStruct((bsz, seq, d), F32),
        compiler_params=_cparams("parallel", "parallel", "arbitrary"),
        name="sb_attention",
    )(qkv3, qkv3, qkv3)


def _mm_res_kernel(a_ref, w_ref, x_ref, g_ref, o_ref):
    m = jnp.dot(a_ref[...].astype(BF16), w_ref[...], preferred_element_type=F32)
    o_ref[...] = x_ref[...] + g_ref[0] * m


def _mm_res(a2, w_bf16, x2, g, seq, tm=512):
    t, d = x2.shape
    k = a2.shape[1]
    return pl.pallas_call(
        _mm_res_kernel,
        grid=(t // tm,),
        in_specs=[pl.BlockSpec((tm, k), lambda i: (i, 0)),
                  pl.BlockSpec((k, d), lambda i: (0, 0)),
                  pl.BlockSpec((tm, d), lambda i: (i, 0)),
                  pl.BlockSpec((1, 1, d), lambda i: ((i * tm) // seq, 0, 0))],
        out_specs=pl.BlockSpec((tm, d), lambda i: (i, 0)),
        out_shape=jax.ShapeDtypeStruct((t, d), F32),
        compiler_params=_cparams("parallel"),
        name="mm_res",
    )(a2, w_bf16, x2, g)


PAIR_ROWS = tuple(PEER_TOPK // (i + 1) for i in range(PEER_TOPK))


def _extract_top(s, payload, k):
    rows = lax.broadcasted_iota(jnp.int32, s.shape, 0)
    big = s.shape[0]
    vals, outs = [], []
    for _ in range(k):
        m = jnp.max(s, axis=0, keepdims=True)
        r = jnp.min(jnp.where(s == m, rows, big), axis=0, keepdims=True)
        sel = rows == r
        vals.append(m)
        if payload is None:
            outs.append(r)
        else:
            outs.append(jnp.max(jnp.where(sel, payload, -1), axis=0, keepdims=True))
        s = jnp.where(sel, NEG_INF, s)
    return jnp.concatenate(vals, axis=0), jnp.concatenate(outs, axis=0)


def _pair_candidates(a, ia, b, ib):
    sub = lax.broadcasted_iota(jnp.int32, (SUBLANES, a.shape[1]), 0)
    cand, ids = [], []
    for i in range(SUBLANES):
        for j0 in range(0, PAIR_ROWS[i], SUBLANES):
            n = min(PAIR_ROWS[i] - j0, SUBLANES)
            c = a[i:i + 1, :] + b[j0:j0 + SUBLANES, :]
            e = ia[i:i + 1, :] * N_KEYS + ib[j0:j0 + SUBLANES, :]
            if n < SUBLANES:
                c = jnp.where(sub < n, c, NEG_INF)
            cand.append(c)
            ids.append(e)
    cand.append(a[SUBLANES:, :] + b[0:1, :])
    ids.append(ia[SUBLANES:, :] * N_KEYS + ib[0:1, :])
    return jnp.concatenate(cand, axis=0), jnp.concatenate(ids, axis=0)


def _peer_route_kernel(row_base, x_ref, gn_ref, sc_ref, sh_ref, wq_ref, keys_ref,
                       h_ref, e_ref, g_ref, q_s, e_s, g_s):
    h = _norm_mod(x_ref[...], gn_ref[...], sc_ref[0], sh_ref[0])
    h_ref[...] = h
    q = jnp.dot(h.astype(BF16), wq_ref[...], preferred_element_type=F32)
    for j in range(2 * PEER_HEADS):
        q_s[j] = q[:, j * PEER_DQH:(j + 1) * PEER_DQH].astype(BF16)
    nt = (((1,), (1,)), ((), ()))

    def head(hd, carry):
        tops = []
        for c in range(2):
            s_t = lax.dot_general(keys_ref[2 * hd + c], q_s[2 * hd + c], nt,
                                  preferred_element_type=F32)
            tops.append(_extract_top(s_t, None, PEER_TOPK))
        cand, ids = _pair_candidates(tops[0][0], tops[0][1], tops[1][0], tops[1][1])
        best, experts = _extract_top(cand, ids, PEER_TOPK)
        ex = jnp.exp(best - best[0:1, :])
        g_s[hd] = ex / jnp.sum(ex, axis=0, keepdims=True)
        e_s[hd] = experts
        return carry

    lax.fori_loop(0, PEER_HEADS, head, 0)
    tm = x_ref.shape[0]
    e_ref[...] = e_s[...].reshape(PEER_HK, tm).T + row_base
    g_ref[...] = g_s[...].reshape(PEER_HK, tm).T


def _peer_route(x2, seq, gn, sc, sh, wq_bf16, keys_bf16, row_base, tm=256):
    t, d = x2.shape
    nq = wq_bf16.shape[1]
    per_seq = lambda i: ((i * tm) // seq, 0, 0)
    return pl.pallas_call(
        functools.partial(_peer_route_kernel, row_base),
        grid=(t // tm,),
        in_specs=[pl.BlockSpec((tm, d), lambda i: (i, 0)),
                  pl.BlockSpec((1, d), lambda i: (0, 0)),
                  pl.BlockSpec((1, 1, d), per_seq),
                  pl.BlockSpec((1, 1, d), per_seq),
                  pl.BlockSpec((d, nq), lambda i: (0, 0)),
                  pl.BlockSpec((2 * PEER_HEADS, N_KEYS, PEER_DQH), lambda i: (0, 0, 0))],
        out_specs=[pl.BlockSpec((tm, d), lambda i: (i, 0)),
                   pl.BlockSpec((tm, PEER_HK), lambda i: (i, 0)),
                   pl.BlockSpec((tm, PEER_HK), lambda i: (i, 0))],
        out_shape=[jax.ShapeDtypeStruct((t, d), F32),
                   jax.ShapeDtypeStruct((t, PEER_HK), jnp.int32),
                   jax.ShapeDtypeStruct((t, PEER_HK), F32)],
        scratch_shapes=[pltpu.VMEM((2 * PEER_HEADS, tm, PEER_DQH), BF16),
                        pltpu.VMEM((PEER_HEADS, PEER_TOPK, tm), jnp.int32),
                        pltpu.VMEM((PEER_HEADS, PEER_TOPK, tm), F32)],
        compiler_params=_cparams("parallel"),
        name="peer_route",
    )(x2, gn.reshape(1, d), sc, sh, wq_bf16, keys_bf16)


SC_ROWS = 16
SC_GROUPS = PEER_HK // SC_ROWS
SC_TB = 32
SC_CHUNKS = D_MODEL // SC_LANES
SC_LCH = LANES // SC_LANES
SC_NBUF = 4


def _table_rows(tab):
    return tab.reshape(-1, D_MODEL // LANES, LANES)


def _sc_mesh():
    return plsc.VectorSubcoreMesh(core_axis_name="c", subcore_axis_name="s")


def _sc_pipeline(n_items, gather, compute):
    ahead = SC_NBUF - 1
    for i in range(ahead):
        gather(i, i).start()

    @pl.loop(0, n_items, step=SC_NBUF)
    def _(it):
        for b in range(SC_NBUF):
            i = it + b
            gather(i, b).wait()

            @pl.when(i + ahead < n_items)
            def _():
                gather(i + ahead, (b + ahead) % SC_NBUF).start()

            compute(i, b)


def _tree_sum(terms):
    while len(terms) > 1:
        terms = [terms[i] + terms[i + 1] for i in range(0, len(terms), 2)]
    return terms[0]


def _peer_pre(h2, idx, u_tab):
    t, d = h2.shape
    info = plsc.get_sparse_core_info()
    nc, nw = info.num_cores, info.num_cores * info.num_subcores
    tpw = t // nw
    n_items = SC_TB * SC_GROUPS

    @functools.partial(
        pl.kernel, mesh=_sc_mesh(),
        out_type=jax.ShapeDtypeStruct((t, PEER_HK), F32),
        scratch_types=[pltpu.VMEM((SC_TB, PEER_HK), jnp.int32),
                       pltpu.VMEM((SC_TB, d), F32),
                       pltpu.VMEM((SC_TB, PEER_HK), F32),
                       pltpu.VMEM((SC_ROWS, SC_LANES), F32)]
                      + [pltpu.VMEM((SC_ROWS,) + u_tab.shape[1:], F32)] * SC_NBUF
                      + [pltpu.SemaphoreType.DMA] * SC_NBUF,
        compiler_params=pltpu.CompilerParams(needs_layout_passes=False),
        name="peer_pre")
    def k(h_hbm, idx_hbm, u_hbm, p_hbm, idx_v, h_v, p_v, tr_v, *ring):
        wid = lax.axis_index("s") * nc + lax.axis_index("c")
        bufs, sems = ring[:SC_NBUF], ring[SC_NBUF:]
        lane = lax.iota(jnp.int32, SC_LANES)

        def gather(it, b):
            tl, g = it // SC_GROUPS, it % SC_GROUPS
            return pltpu.make_async_copy(
                u_hbm.at[idx_v.at[tl, pl.ds(g * SC_ROWS, SC_ROWS)]], bufs[b], sems[b])

        def compute(it, b):
            tl, g = it // SC_GROUPS, it % SC_GROUPS
            buf = bufs[b]

            def body(c, accs):
                xc = h_v[tl, pl.ds(c * SC_LANES, SC_LANES)]
                sub, sl = c // SC_LCH, pl.ds((c % SC_LCH) * SC_LANES, SC_LANES)
                return tuple(accs[r] + buf[r, sub, sl] * xc for r in range(SC_ROWS))

            zero = jnp.zeros((SC_LANES,), F32)
            accs = lax.fori_loop(0, SC_CHUNKS, body, (zero,) * SC_ROWS)
            for r in range(SC_ROWS):
                tr_v[r, :] = accs[r]
            cols = [plsc.load_gather(tr_v, [lane, jnp.full((SC_LANES,), j, jnp.int32)])
                    for j in range(SC_LANES)]
            p_v[tl, pl.ds(g * SC_ROWS, SC_ROWS)] = _tree_sum(cols)

        @pl.loop(0, tpw // SC_TB)
        def _(blk):
            t0 = wid * tpw + blk * SC_TB
            pltpu.sync_copy(idx_hbm.at[pl.ds(t0, SC_TB)], idx_v)
            pltpu.sync_copy(h_hbm.at[pl.ds(t0, SC_TB)], h_v)
            _sc_pipeline(n_items, gather, compute)
            pltpu.sync_copy(p_v, p_hbm.at[pl.ds(t0, SC_TB)])

    return k(h2, idx, u_tab)


def _peer_post(act, idx, v_tab):
    t = act.shape[0]
    d = v_tab.shape[1] * v_tab.shape[2]
    info = plsc.get_sparse_core_info()
    nc, nw = info.num_cores, info.num_cores * info.num_subcores
    tpw = t // nw
    n_items = SC_TB * SC_GROUPS

    @functools.partial(
        pl.kernel, mesh=_sc_mesh(),
        out_type=jax.ShapeDtypeStruct((t, d), F32),
        scratch_types=[pltpu.VMEM((SC_TB, PEER_HK), jnp.int32),
                       pltpu.VMEM((SC_TB, PEER_HK), F32),
                       pltpu.VMEM((SC_TB, d), F32)]
                      + [pltpu.VMEM((SC_ROWS,) + v_tab.shape[1:], F32)] * SC_NBUF
                      + [pltpu.SemaphoreType.DMA] * SC_NBUF,
        compiler_params=pltpu.CompilerParams(needs_layout_passes=False),
        name="peer_post")
    def k(a_hbm, idx_hbm, v_hbm, y_hbm, idx_v, a_v, y_v, *ring):
        wid = lax.axis_index("s") * nc + lax.axis_index("c")
        bufs, sems = ring[:SC_NBUF], ring[SC_NBUF:]

        def gather(it, b):
            tl, g = it // SC_GROUPS, it % SC_GROUPS
            return pltpu.make_async_copy(
                v_hbm.at[idx_v.at[tl, pl.ds(g * SC_ROWS, SC_ROWS)]], bufs[b], sems[b])

        def compute(it, b):
            tl, g = it // SC_GROUPS, it % SC_GROUPS
            buf = bufs[b]
            tlv = jnp.full((SC_LANES,), tl, jnp.int32)
            w = tuple(plsc.load_gather(a_v, [tlv, jnp.full((SC_LANES,), g * SC_ROWS + r, jnp.int32)])
                      for r in range(SC_ROWS))

            @plsc.parallel_loop(0, SC_CHUNKS, unroll=2)
            def _(c):
                sub, sl = c // SC_LCH, pl.ds((c % SC_LCH) * SC_LANES, SC_LANES)
                plsc.addupdate(y_v.at[tl, pl.ds(c * SC_LANES, SC_LANES)],
                               _tree_sum([w[r] * buf[r, sub, sl] for r in range(SC_ROWS)]))

        @pl.loop(0, tpw // SC_TB)
        def _(blk):
            t0 = wid * tpw + blk * SC_TB
            pltpu.sync_copy(idx_hbm.at[pl.ds(t0, SC_TB)], idx_v)
            pltpu.sync_copy(a_hbm.at[pl.ds(t0, SC_TB)], a_v)

            @pl.loop(0, SC_TB)
            def _(r):
                @plsc.parallel_loop(0, SC_CHUNKS, unroll=4)
                def _(c):
                    y_v[r, pl.ds(c * SC_LANES, SC_LANES)] = jnp.zeros((SC_LANES,), F32)

            _sc_pipeline(n_items, gather, compute)
            pltpu.sync_copy(y_v, y_hbm.at[pl.ds(t0, SC_TB)])

    return k(act, idx, v_tab)


def _act_kernel(p_ref, g_ref, o_ref):
    p = p_ref[...]
    o_ref[...] = (p * (lax.erf(p * (1.0 / math.sqrt(2.0))) + 1.0) * 0.5) * g_ref[...]


def _expert_act(p, gates, tm=2048):
    t, n = p.shape
    spec = pl.BlockSpec((tm, n), lambda i: (i, 0))
    return pl.pallas_call(
        _act_kernel, grid=(t // tm,), in_specs=[spec, spec], out_specs=spec,
        out_shape=jax.ShapeDtypeStruct((t, n), F32),
        compiler_params=_cparams("parallel"), name="expert_act",
    )(p, gates)


def _res_kernel(x_ref, y_ref, g_ref, o_ref):
    o_ref[...] = x_ref[...] + g_ref[0] * y_ref[...]


def _res_norm_kernel(x_ref, y_ref, g_ref, fg_ref, o_ref):
    x = x_ref[...] + g_ref[0] * y_ref[...]
    ms = jnp.mean(x * x, axis=-1, keepdims=True)
    o_ref[...] = x * lax.rsqrt(ms + EPS) * fg_ref[...]


def _residual(x2, y2, g, seq, final_g=None, tm=512):
    t, d = x2.shape
    tile = pl.BlockSpec((tm, d), lambda i: (i, 0))
    gspec = pl.BlockSpec((1, 1, d), lambda i: ((i * tm) // seq, 0, 0))
    if final_g is None:
        kern, ins, args = _res_kernel, [tile, tile, gspec], (x2, y2, g)
    else:
        kern = _res_norm_kernel
        ins = [tile, tile, gspec, pl.BlockSpec((1, d), lambda i: (0, 0))]
        args = (x2, y2, g, final_g.reshape(1, d))
    return pl.pallas_call(
        kern, grid=(t // tm,), in_specs=ins, out_specs=tile,
        out_shape=jax.ShapeDtypeStruct((t, d), F32),
        compiler_params=_cparams("parallel"), name="residual",
    )(*args)


def _peer(x2, seq, gn, sc, sh, wq_bf16, keys_bf16, u_all, v_all, layer):
    rows = _peer_route(x2, seq, gn, sc, sh, wq_bf16, keys_bf16, layer * N_KEYS * N_KEYS)
    h2, experts, gates = rows
    p = _peer_pre(h2, experts, u_all)
    act = _expert_act(p, gates)
    return _peer_post(act, experts, v_all)


BATCH_SPLIT = 4


def kernel(x, c, ada_w, ada_b, norm_mix_g, norm_ffn_g, cv_w_in, cv_b_in, cv_w_dw, cv_b_dw, cv_ln_g, cv_ln_b, cv_w_out, cv_b_out, sb_w_qkv, sb_w_o, pk_w_q, pk_keys, pk_u, pk_v, final_g):
    bsz, seq, d = x.shape
    mod = _ada_mod(c, ada_w, ada_b)
    w_in, w_out = cv_w_in.astype(BF16), cv_w_out.astype(BF16)
    w_qkv, w_o = sb_w_qkv.astype(BF16), sb_w_o.astype(BF16)
    w_q = pk_w_q.astype(BF16)
    keys = pk_keys.reshape(DEPTH, 2 * PEER_HEADS, N_KEYS, PEER_DQH).astype(BF16)
    u_all, v_all = _table_rows(pk_u), _table_rows(pk_v)
    nb = bsz // BATCH_SPLIT
    t = nb * seq
    outs = []
    for part in range(BATCH_SPLIT):
        lo = part * nb
        x2 = x[lo:lo + nb].reshape(t, d)
        for i in range(DEPTH):
            sh1, sc1, g1, sh2, sc2, g2 = (
                mod[i, lo:lo + nb, n * d:(n + 1) * d].reshape(nb, 1, d) for n in range(ADA_CHUNKS))
            j = i // 2
            if i % 2 == 0:
                u2 = _norm_glu(x2, seq, norm_mix_g[i], sc1, sh1, w_in[j], cv_b_in[j])
                x2 = _conv_block(u2.reshape(nb, seq, d), x2.reshape(nb, seq, d), g1,
                                 cv_w_dw[j], cv_b_dw[j], cv_ln_g[j], cv_ln_b[j],
                                 w_out[j], cv_b_out[j]).reshape(t, d)
            else:
                qkv = _norm_mm(x2, seq, norm_mix_g[i], sc1, sh1, w_qkv[j], BF16)
                o = _attention(qkv.reshape(nb, seq, 3 * d))
                x2 = _mm_res(o.reshape(t, d), w_o[j], x2, g1, seq)
            y2 = _peer(x2, seq, norm_ffn_g[i], sc2, sh2, w_q[i], keys[i], u_all, v_all, i)
            x2 = _residual(x2, y2, g2, seq, final_g if i == DEPTH - 1 else None)
        outs.append(x2.reshape(nb, seq, d))
    return jnp.concatenate(outs, axis=0)
```

```python
import functools
import math

import jax
import jax.numpy as jnp
from jax import lax
from jax.experimental import pallas as pl
from jax.experimental.pallas import tpu as pltpu
from jax.experimental.pallas import tpu_sc as plsc

D_MODEL = 1024
DEPTH = 2
CONV_WIDTH = 31
N_HEADS = 16
HEAD_DIM = 64
PEER_HEADS = 8
PEER_DQH = 128
N_KEYS = 128
PEER_TOPK = 16
PEER_HK = PEER_HEADS * PEER_TOPK
ADA_CHUNKS = 6
EPS = 1e-6

LANES = 128
SUBLANES = 8
SC_LANES = 16
VMEM_LIMIT = 48 * 1024 * 1024

F32 = jnp.float32
BF16 = jnp.bfloat16
NEG_INF = float("-inf")


def _cparams(*sem):
    return pltpu.CompilerParams(dimension_semantics=sem, vmem_limit_bytes=VMEM_LIMIT)


def _norm_mod(x, gn, sc, sh):
    ms = jnp.mean(x * x, axis=-1, keepdims=True)
    return (x * lax.rsqrt(ms + EPS)) * gn * (1.0 + sc) + sh


def _ada_kernel(c_ref, w_ref, b_ref, o_ref):
    c = c_ref[...]
    ca = c * jax.nn.sigmoid(c)
    o_ref[0] = jnp.dot(ca.astype(BF16), w_ref[0].astype(BF16),
                       preferred_element_type=F32) + b_ref[0]


def _ada_mod(c, ada_w, ada_b):
    depth, d, n = ada_w.shape
    bsz = c.shape[0]
    tn = 512
    return pl.pallas_call(
        _ada_kernel,
        grid=(depth, n // tn),
        in_specs=[pl.BlockSpec((bsz, d), lambda i, j: (0, 0)),
                  pl.BlockSpec((1, d, tn), lambda i, j: (i, 0, j)),
                  pl.BlockSpec((1, 1, tn), lambda i, j: (i, 0, j))],
        out_specs=pl.BlockSpec((1, bsz, tn), lambda i, j: (i, 0, j)),
        out_shape=jax.ShapeDtypeStruct((depth, bsz, n), F32),
        compiler_params=_cparams("parallel", "parallel"),
        name="ada_mod",
    )(c, ada_w, ada_b.reshape(depth, 1, n))


def _norm_mm_kernel(x_ref, gn_ref, sc_ref, sh_ref, w_ref, o_ref):
    h = _norm_mod(x_ref[...], gn_ref[...], sc_ref[0], sh_ref[0])
    o_ref[...] = jnp.dot(h.astype(BF16), w_ref[...],
                         preferred_element_type=F32).astype(o_ref.dtype)


def _norm_mm(x2, seq, gn, sc, sh, w_bf16, out_dtype, tm=256):
    t, d = x2.shape
    n = w_bf16.shape[1]
    per_seq = lambda i: ((i * tm) // seq, 0, 0)
    return pl.pallas_call(
        _norm_mm_kernel,
        grid=(t // tm,),
        in_specs=[pl.BlockSpec((tm, d), lambda i: (i, 0)),
                  pl.BlockSpec((1, d), lambda i: (0, 0)),
                  pl.BlockSpec((1, 1, d), per_seq),
                  pl.BlockSpec((1, 1, d), per_seq),
                  pl.BlockSpec((d, n), lambda i: (0, 0))],
        out_specs=pl.BlockSpec((tm, n), lambda i: (i, 0)),
        out_shape=jax.ShapeDtypeStruct((t, n), out_dtype),
        compiler_params=_cparams("parallel"),
        name="norm_mm",
    )(x2, gn.reshape(1, d), sc, sh, w_bf16)


def _norm_glu_kernel(x_ref, gn_ref, sc_ref, sh_ref, wa_ref, wg_ref, ba_ref, bg_ref, o_ref):
    h = _norm_mod(x_ref[...], gn_ref[...], sc_ref[0], sh_ref[0]).astype(BF16)
    a = jnp.dot(h, wa_ref[...], preferred_element_type=F32) + ba_ref[...]
    g = jnp.dot(h, wg_ref[...], preferred_element_type=F32) + bg_ref[...]
    o_ref[...] = a * jax.nn.sigmoid(g)


def _norm_glu(x2, seq, gn, sc, sh, w_in_bf16, b_in, tm=256):
    t, d = x2.shape
    n = w_in_bf16.shape[1] // 2
    per_seq = lambda i: ((i * tm) // seq, 0, 0)
    b2 = b_in.reshape(1, 2 * n)
    return pl.pallas_call(
        _norm_glu_kernel,
        grid=(t // tm,),
        in_specs=[pl.BlockSpec((tm, d), lambda i: (i, 0)),
                  pl.BlockSpec((1, d), lambda i: (0, 0)),
                  pl.BlockSpec((1, 1, d), per_seq),
                  pl.BlockSpec((1, 1, d), per_seq),
                  pl.BlockSpec((d, n), lambda i: (0, 0)),
                  pl.BlockSpec((d, n), lambda i: (0, 1)),
                  pl.BlockSpec((1, n), lambda i: (0, 0)),
                  pl.BlockSpec((1, n), lambda i: (0, 1))],
        out_specs=pl.BlockSpec((tm, n), lambda i: (i, 0)),
        out_shape=jax.ShapeDtypeStruct((t, n), F32),
        compiler_params=_cparams("parallel"),
        name="norm_glu",
    )(x2, gn.reshape(1, d), sc, sh, w_in_bf16, w_in_bf16, b2, b2)


HALO = 32


def _conv_kernel(cur_ref, prev_ref, wdw_ref, bdw_ref, lng_ref, lnb_ref, wout_ref,
                 bout_ref, x_ref, g_ref, o_ref, win_s):
    ts = cur_ref.shape[1]
    first = pl.program_id(1) == 0
    win_s[0:HALO, :] = jnp.where(first, 0.0, prev_ref[0])
    win_s[HALO:HALO + ts, :] = cur_ref[0]
    acc = jnp.zeros(cur_ref.shape[1:], F32) + bdw_ref[...]
    off = HALO - (CONV_WIDTH - 1)
    for k in range(CONV_WIDTH):
        acc = acc + win_s[off + k:off + k + ts, :] * wdw_ref[k:k + 1, :]
    mu = jnp.mean(acc, axis=-1, keepdims=True)
    cen = acc - mu
    var = jnp.mean(cen * cen, axis=-1, keepdims=True)
    y = cen * lax.rsqrt(var + EPS) * lng_ref[...] + lnb_ref[...]
    y = y * jax.nn.sigmoid(y)
    m = jnp.dot(y.astype(BF16), wout_ref[...], preferred_element_type=F32) + bout_ref[...]
    o_ref[0] = x_ref[0] + g_ref[0] * m


def _conv_block(u3, x3, g1, w_dw, b_dw, ln_g, ln_b, w_out_bf16, b_out, ts=256):
    bsz, seq, d = u3.shape
    r = ts // HALO
    wdw = jnp.concatenate([w_dw, jnp.zeros((HALO - CONV_WIDTH, d), F32)], axis=0)
    vec = lambda b, i: (0, 0)
    return pl.pallas_call(
        _conv_kernel,
        grid=(bsz, seq // ts),
        in_specs=[pl.BlockSpec((1, ts, d), lambda b, i: (b, i, 0)),
                  pl.BlockSpec((1, HALO, d), lambda b, i: (b, jnp.maximum(i * r - 1, 0), 0)),
                  pl.BlockSpec((HALO, d), vec),
                  pl.BlockSpec((1, d), vec),
                  pl.BlockSpec((1, d), vec),
                  pl.BlockSpec((1, d), vec),
                  pl.BlockSpec((d, d), vec),
                  pl.BlockSpec((1, d), vec),
                  pl.BlockSpec((1, ts, d), lambda b, i: (b, i, 0)),
                  pl.BlockSpec((1, 1, d), lambda b, i: (b, 0, 0))],
        out_specs=pl.BlockSpec((1, ts, d), lambda b, i: (b, i, 0)),
        out_shape=jax.ShapeDtypeStruct((bsz, seq, d), F32),
        scratch_shapes=[pltpu.VMEM((HALO + ts, d), F32)],
        compiler_params=_cparams("parallel", "arbitrary"),
        name="conv_block",
    )(u3, u3, wdw, b_dw.reshape(1, d), ln_g.reshape(1, d), ln_b.reshape(1, d),
      w_out_bf16, b_out.reshape(1, d), x3, g1)


ATT_T = 256


def _attn_kernel(q_ref, k_ref, v_ref, o_ref):
    t = ATT_T
    qi = pl.program_id(2)
    q2 = q_ref[0] * (1.0 / math.sqrt(HEAD_DIM))
    lane = lax.broadcasted_iota(jnp.int32, (1, LANES), 1)
    head_mask = (lane < HEAD_DIM, lane >= HEAD_DIM)
    row = lax.broadcasted_iota(jnp.int32, (t, t), 0)
    col = lax.broadcasted_iota(jnp.int32, (t, t), 1)
    tri = (row >= col).astype(BF16)
    tri2 = jnp.concatenate([tri, tri], axis=0)
    qh = tuple(jnp.where(m, q2, jnp.zeros_like(q2)) for m in head_mask)
    nt = (((1,), (1,)), ((), ()))

    def tile(kt, suf, acc, diag):
        start = pl.multiple_of(kt * t, t)
        k2 = k_ref[0, pl.ds(start, t), :]
        v2 = v_ref[0, pl.ds(start, t), :]
        new_suf = []
        for h in range(2):
            z = lax.dot_general(qh[h], k2, nt, preferred_element_type=F32)
            sp = jnp.maximum(z, 0.0) + jnp.log(1.0 + jnp.exp(-jnp.abs(z)))
            if diag:
                sp = jnp.where(col < row, sp, 0.0)
            hi = sp.astype(BF16)
            mid = (sp - hi.astype(F32)).astype(BF16)
            cs = jnp.dot(jnp.concatenate([hi, mid], axis=1), tri2,
                         preferred_element_type=F32)
            a = jnp.exp(z - cs - suf[h])
            if diag:
                a = jnp.where(col < row, a, 0.0)
            pv = jnp.dot(a.astype(BF16), v2, preferred_element_type=F32)
            acc = acc + jnp.where(head_mask[h], pv, 0.0)
            new_suf.append(suf[h] + cs[:, 0:1])
        return tuple(new_suf), acc

    zero = jnp.zeros((t, 1), F32)
    suf, acc = tile(qi, (zero, zero), jnp.zeros((t, LANES), F32), True)

    def body(n, carry):
        s0, s1, acc = carry
        (s0, s1), acc = tile(qi - n, (s0, s1), acc, False)
        return s0, s1, acc

    _, _, acc = lax.fori_loop(1, qi + 1, body, (suf[0], suf[1], acc))
    o_ref[0] = acc


def _attention(qkv3):
    bsz, seq, n3 = qkv3.shape
    d = n3 // 3
    nb = d // LANES
    return pl.pallas_call(
        _attn_kernel,
        grid=(bsz, nb, seq // ATT_T),
        in_specs=[pl.BlockSpec((1, ATT_T, LANES), lambda b, h, i: (b, i, h)),
                  pl.BlockSpec((1, seq, LANES), lambda b, h, i: (b, 0, nb + h)),
                  pl.BlockSpec((1, seq, LANES), lambda b, h, i: (b, 0, 2 * nb + h))],
        out_specs=pl.BlockSpec((1, ATT_T, LANES), lambda b, h, i: (b, i, h)),
        out_shape=jax.ShapeDtypeStruct((bsz, seq, d), F32),
        compiler_params=_cparams("parallel", "parallel", "arbitrary"),
        name="sb_attention",
    )(qkv3, qkv3, qkv3)


def _mm_res_kernel(a_ref, w_ref, x_ref, g_ref, o_ref):
    m = jnp.dot(a_ref[...].astype(BF16), w_ref[...], preferred_element_type=F32)
    o_ref[...] = x_ref[...] + g_ref[0] * m


def _mm_res(a2, w_bf16, x2, g, seq, tm=512):
    t, d = x2.shape
    k = a2.shape[1]
    return pl.pallas_call(
        _mm_res_kernel,
        grid=(t // tm,),
        in_specs=[pl.BlockSpec((tm, k), lambda i: (i, 0)),
                  pl.BlockSpec((k, d), lambda i: (0, 0)),
                  pl.BlockSpec((tm, d), lambda i: (i, 0)),
                  pl.BlockSpec((1, 1, d), lambda i: ((i * tm) // seq, 0, 0))],
        out_specs=pl.BlockSpec((tm, d), lambda i: (i, 0)),
        out_shape=jax.ShapeDtypeStruct((t, d), F32),
        compiler_params=_cparams("parallel"),
        name="mm_res",
    )(a2, w_bf16, x2, g)


PAIR_ROWS = tuple(PEER_TOPK // (i + 1) for i in range(PEER_TOPK))


def _extract_top(s, payload, k):
    rows = lax.broadcasted_iota(jnp.int32, s.shape, 0)
    big = s.shape[0]
    vals, outs = [], []
    for _ in range(k):
        m = jnp.max(s, axis=0, keepdims=True)
        r = jnp.min(jnp.where(s == m, rows, big), axis=0, keepdims=True)
        sel = rows == r
        vals.append(m)
        if payload is None:
            outs.append(r)
        else:
            outs.append(jnp.max(jnp.where(sel, payload, -1), axis=0, keepdims=True))
        s = jnp.where(sel, NEG_INF, s)
    return jnp.concatenate(vals, axis=0), jnp.concatenate(outs, axis=0)


def _pair_candidates(a, ia, b, ib):
    sub = lax.broadcasted_iota(jnp.int32, (SUBLANES, a.shape[1]), 0)
    cand, ids = [], []
    for i in range(SUBLANES):
        for j0 in range(0, PAIR_ROWS[i], SUBLANES):
            n = min(PAIR_ROWS[i] - j0, SUBLANES)
            c = a[i:i + 1, :] + b[j0:j0 + SUBLANES, :]
            e = ia[i:i + 1, :] * N_KEYS + ib[j0:j0 + SUBLANES, :]
            if n < SUBLANES:
                c = jnp.where(sub < n, c, NEG_INF)
            cand.append(c)
            ids.append(e)
    cand.append(a[SUBLANES:, :] + b[0:1, :])
    ids.append(ia[SUBLANES:, :] * N_KEYS + ib[0:1, :])
    return jnp.concatenate(cand, axis=0), jnp.concatenate(ids, axis=0)


def _peer_route_kernel(row_base, x_ref, gn_ref, sc_ref, sh_ref, wq_ref, keys_ref,
                       h_ref, e_ref, g_ref, q_s, e_s, g_s):
    h = _norm_mod(x_ref[...], gn_ref[...], sc_ref[0], sh_ref[0])
    h_ref[...] = h
    q = jnp.dot(h.astype(BF16), wq_ref[...], preferred_element_type=F32)
    for j in range(2 * PEER_HEADS):
        q_s[j] = q[:, j * PEER_DQH:(j + 1) * PEER_DQH].astype(BF16)
    nt = (((1,), (1,)), ((), ()))

    def head(hd, carry):
        tops = []
        for c in range(2):
            s_t = lax.dot_general(keys_ref[2 * hd + c], q_s[2 * hd + c], nt,
                                  preferred_element_type=F32)
            tops.append(_extract_top(s_t, None, PEER_TOPK))
        cand, ids = _pair_candidates(tops[0][0], tops[0][1], tops[1][0], tops[1][1])
        best, experts = _extract_top(cand, ids, PEER_TOPK)
        ex = jnp.exp(best - best[0:1, :])
        g_s[hd] = ex / jnp.sum(ex, axis=0, keepdims=True)
        e_s[hd] = experts
        return carry

    lax.fori_loop(0, PEER_HEADS, head, 0)
    tm = x_ref.shape[0]
    e_ref[...] = e_s[...].reshape(PEER_HK, tm).T + row_base
    g_ref[...] = g_s[...].reshape(PEER_HK, tm).T


def _peer_route(x2, seq, gn, sc, sh, wq_bf16, keys_bf16, row_base, tm=256):
    t, d = x2.shape
    nq = wq_bf16.shape[1]
    per_seq = lambda i: ((i * tm) // seq, 0, 0)
    return pl.pallas_call(
        functools.partial(_peer_route_kernel, row_base),
        grid=(t // tm,),
        in_specs=[pl.BlockSpec((tm, d), lambda i: (i, 0)),
                  pl.BlockSpec((1, d), lambda i: (0, 0)),
                  pl.BlockSpec((1, 1, d), per_seq),
                  pl.BlockSpec((1, 1, d), per_seq),
                  pl.BlockSpec((d, nq), lambda i: (0, 0)),
                  pl.BlockSpec((2 * PEER_HEADS, N_KEYS, PEER_DQH), lambda i: (0, 0, 0))],
        out_specs=[pl.BlockSpec((tm, d), lambda i: (i, 0)),
                   pl.BlockSpec((tm, PEER_HK), lambda i: (i, 0)),
                   pl.BlockSpec((tm, PEER_HK), lambda i: (i, 0))],
        out_shape=[jax.ShapeDtypeStruct((t, d), F32),
                   jax.ShapeDtypeStruct((t, PEER_HK), jnp.int32),
                   jax.ShapeDtypeStruct((t, PEER_HK), F32)],
        scratch_shapes=[pltpu.VMEM((2 * PEER_HEADS, tm, PEER_DQH), BF16),
                        pltpu.VMEM((PEER_HEADS, PEER_TOPK, tm), jnp.int32),
                        pltpu.VMEM((PEER_HEADS, PEER_TOPK, tm), F32)],
        compiler_params=_cparams("parallel"),
        name="peer_route",
    )(x2, gn.reshape(1, d), sc, sh, wq_bf16, keys_bf16)


SC_ROWS = 16
SC_GROUPS = PEER_HK // SC_ROWS
SC_TB = 32
SC_CHUNKS = D_MODEL // SC_LANES
SC_HALF = D_MODEL // 2
SC_WCH = SC_HALF // SC_LANES
SC_NBUF = 4


def _pack_table(tab):
    b = lax.bitcast_convert_type(tab.reshape(-1, D_MODEL).astype(BF16), jnp.uint16)
    b = b.astype(jnp.uint32)
    return lax.bitcast_convert_type(b[:, :SC_HALF] | (b[:, SC_HALF:] << 16), jnp.int32)


def _unpack_pair(w):
    lo = plsc.bitcast(lax.shift_left(w, jnp.int32(16)), F32)
    hi = plsc.bitcast(w & jnp.int32(-65536), F32)
    return lo, hi


def _sc_mesh():
    return plsc.VectorSubcoreMesh(core_axis_name="c", subcore_axis_name="s")


def _sc_pipeline(n_items, gather, compute):
    ahead = SC_NBUF - 1
    for i in range(ahead):
        gather(i, i).start()

    @pl.loop(0, n_items, step=SC_NBUF)
    def _(it):
        for b in range(SC_NBUF):
            i = it + b
            gather(i, b).wait()

            @pl.when(i + ahead < n_items)
            def _():
                gather(i + ahead, (b + ahead) % SC_NBUF).start()

            compute(i, b)


def _tree_sum(terms):
    while len(terms) > 1:
        terms = [terms[i] + terms[i + 1] for i in range(0, len(terms), 2)]
    return terms[0]


def _peer_pre(h2, idx, u_tab):
    t, d = h2.shape
    info = plsc.get_sparse_core_info()
    nc, nw = info.num_cores, info.num_cores * info.num_subcores
    tpw = t // nw
    n_items = SC_TB * SC_GROUPS

    @functools.partial(
        pl.kernel, mesh=_sc_mesh(),
        out_type=jax.ShapeDtypeStruct((t, PEER_HK), F32),
        scratch_types=[pltpu.VMEM((SC_TB, PEER_HK), jnp.int32),
                       pltpu.VMEM((SC_TB, d), F32),
                       pltpu.VMEM((SC_TB, PEER_HK), F32),
                       pltpu.VMEM((SC_ROWS, SC_LANES), F32)]
                      + [pltpu.VMEM((SC_ROWS, SC_HALF), jnp.int32)] * SC_NBUF
                      + [pltpu.SemaphoreType.DMA] * SC_NBUF,
        compiler_params=pltpu.CompilerParams(needs_layout_passes=False),
        name="peer_pre")
    def k(h_hbm, idx_hbm, u_hbm, p_hbm, idx_v, h_v, p_v, tr_v, *ring):
        wid = lax.axis_index("s") * nc + lax.axis_index("c")
        bufs, sems = ring[:SC_NBUF], ring[SC_NBUF:]
        lane = lax.iota(jnp.int32, SC_LANES)

        def gather(it, b):
            tl, g = it // SC_GROUPS, it % SC_GROUPS
            return pltpu.make_async_copy(
                u_hbm.at[idx_v.at[tl, pl.ds(g * SC_ROWS, SC_ROWS)]], bufs[b], sems[b])

        def compute(it, b):
            tl, g = it // SC_GROUPS, it % SC_GROUPS
            buf = bufs[b]

            def body(c, accs):
                sl = pl.ds(c * SC_LANES, SC_LANES)
                x_lo = h_v[tl, sl]
                x_hi = h_v[tl, pl.ds(SC_HALF + c * SC_LANES, SC_LANES)]
                out = []
                for r in range(SC_ROWS):
                    lo, hi = _unpack_pair(buf[r, sl])
                    out.append(accs[r] + (lo * x_lo + hi * x_hi))
                return tuple(out)

            zero = jnp.zeros((SC_LANES,), F32)
            accs = lax.fori_loop(0, SC_WCH, body, (zero,) * SC_ROWS)
            for r in range(SC_ROWS):
                tr_v[r, :] = accs[r]
            cols = [plsc.load_gather(tr_v, [lane, jnp.full((SC_LANES,), j, jnp.int32)])
                    for j in range(SC_LANES)]
            p_v[tl, pl.ds(g * SC_ROWS, SC_ROWS)] = _tree_sum(cols)

        @pl.loop(0, tpw // SC_TB)
        def _(blk):
            t0 = wid * tpw + blk * SC_TB
            pltpu.sync_copy(idx_hbm.at[pl.ds(t0, SC_TB)], idx_v)
            pltpu.sync_copy(h_hbm.at[pl.ds(t0, SC_TB)], h_v)
            _sc_pipeline(n_items, gather, compute)
            pltpu.sync_copy(p_v, p_hbm.at[pl.ds(t0, SC_TB)])

    return k(h2, idx, u_tab)


def _peer_post(act, idx, v_tab):
    t = act.shape[0]
    d = 2 * v_tab.shape[1]
    info = plsc.get_sparse_core_info()
    nc, nw = info.num_cores, info.num_cores * info.num_subcores
    tpw = t // nw
    n_items = SC_TB * SC_GROUPS

    @functools.partial(
        pl.kernel, mesh=_sc_mesh(),
        out_type=jax.ShapeDtypeStruct((t, d), F32),
        scratch_types=[pltpu.VMEM((SC_TB, PEER_HK), jnp.int32),
                       pltpu.VMEM((SC_TB, PEER_HK), F32),
                       pltpu.VMEM((SC_TB, d), F32)]
                      + [pltpu.VMEM((SC_ROWS, SC_HALF), jnp.int32)] * SC_NBUF
                      + [pltpu.SemaphoreType.DMA] * SC_NBUF,
        compiler_params=pltpu.CompilerParams(needs_layout_passes=False),
        name="peer_post")
    def k(a_hbm, idx_hbm, v_hbm, y_hbm, idx_v, a_v, y_v, *ring):
        wid = lax.axis_index("s") * nc + lax.axis_index("c")
        bufs, sems = ring[:SC_NBUF], ring[SC_NBUF:]

        def gather(it, b):
            tl, g = it // SC_GROUPS, it % SC_GROUPS
            return pltpu.make_async_copy(
                v_hbm.at[idx_v.at[tl, pl.ds(g * SC_ROWS, SC_ROWS)]], bufs[b], sems[b])

        def compute(it, b):
            tl, g = it // SC_GROUPS, it % SC_GROUPS
            buf = bufs[b]
            tlv = jnp.full((SC_LANES,), tl, jnp.int32)
            w = tuple(plsc.load_gather(a_v, [tlv, jnp.full((SC_LANES,), g * SC_ROWS + r, jnp.int32)])
                      for r in range(SC_ROWS))

            @plsc.parallel_loop(0, SC_WCH, unroll=2)
            def _(c):
                sl = pl.ds(c * SC_LANES, SC_LANES)
                parts = [_unpack_pair(buf[r, sl]) for r in range(SC_ROWS)]
                plsc.addupdate(y_v.at[tl, sl],
                               _tree_sum([w[r] * parts[r][0] for r in range(SC_ROWS)]))
                plsc.addupdate(y_v.at[tl, pl.ds(SC_HALF + c * SC_LANES, SC_LANES)],
                               _tree_sum([w[r] * parts[r][1] for r in range(SC_ROWS)]))

        @pl.loop(0, tpw // SC_TB)
        def _(blk):
            t0 = wid * tpw + blk * SC_TB
            pltpu.sync_copy(idx_hbm.at[pl.ds(t0, SC_TB)], idx_v)
            pltpu.sync_copy(a_hbm.at[pl.ds(t0, SC_TB)], a_v)

            @pl.loop(0, SC_TB)
            def _(r):
                @plsc.parallel_loop(0, SC_CHUNKS, unroll=4)
                def _(c):
                    y_v[r, pl.ds(c * SC_LANES, SC_LANES)] = jnp.zeros((SC_LANES,), F32)

            _sc_pipeline(n_items, gather, compute)
            pltpu.sync_copy(y_v, y_hbm.at[pl.ds(t0, SC_TB)])

    return k(act, idx, v_tab)


def _act_kernel(p_ref, g_ref, o_ref):
    p = p_ref[...]
    o_ref[...] = (p * (lax.erf(p * (1.0 / math.sqrt(2.0))) + 1.0) * 0.5) * g_ref[...]


def _expert_act(p, gates, tm=2048):
    t, n = p.shape
    spec = pl.BlockSpec((tm, n), lambda i: (i, 0))
    return pl.pallas_call(
        _act_kernel, grid=(t // tm,), in_specs=[spec, spec], out_specs=spec,
        out_shape=jax.ShapeDtypeStruct((t, n), F32),
        compiler_params=_cparams("parallel"), name="expert_act",
    )(p, gates)


def _res_kernel(x_ref, y_ref, g_ref, o_ref):
    o_ref[...] = x_ref[...] + g_ref[0] * y_ref[...]


def _res_norm_kernel(x_ref, y_ref, g_ref, fg_ref, o_ref):
    x = x_ref[...] + g_ref[0] * y_ref[...]
    ms = jnp.mean(x * x, axis=-1, keepdims=True)
    o_ref[...] = x * lax.rsqrt(ms + EPS) * fg_ref[...]


def _residual(x2, y2, g, seq, final_g=None, tm=512):
    t, d = x2.shape
    tile = pl.BlockSpec((tm, d), lambda i: (i, 0))
    gspec = pl.BlockSpec((1, 1, d), lambda i: ((i * tm) // seq, 0, 0))
    if final_g is None:
        kern, ins, args = _res_kernel, [tile, tile, gspec], (x2, y2, g)
    else:
        kern = _res_norm_kernel
        ins = [tile, tile, gspec, pl.BlockSpec((1, d), lambda i: (0, 0))]
        args = (x2, y2, g, final_g.reshape(1, d))
    return pl.pallas_call(
        kern, grid=(t // tm,), in_specs=ins, out_specs=tile,
        out_shape=jax.ShapeDtypeStruct((t, d), F32),
        compiler_params=_cparams("parallel"), name="residual",
    )(*args)


def _peer(x2, seq, gn, sc, sh, wq_bf16, keys_bf16, u_all, v_all, layer):
    rows = _peer_route(x2, seq, gn, sc, sh, wq_bf16, keys_bf16, layer * N_KEYS * N_KEYS)
    h2, experts, gates = rows
    p = _peer_pre(h2, experts, u_all)
    act = _expert_act(p, gates)
    return _peer_post(act, experts, v_all)


BATCH_SPLIT = 4


def kernel(x, c, ada_w, ada_b, norm_mix_g, norm_ffn_g, cv_w_in, cv_b_in, cv_w_dw, cv_b_dw, cv_ln_g, cv_ln_b, cv_w_out, cv_b_out, sb_w_qkv, sb_w_o, pk_w_q, pk_keys, pk_u, pk_v, final_g):
    bsz, seq, d = x.shape
    mod = _ada_mod(c, ada_w, ada_b)
    w_in, w_out = cv_w_in.astype(BF16), cv_w_out.astype(BF16)
    w_qkv, w_o = sb_w_qkv.astype(BF16), sb_w_o.astype(BF16)
    w_q = pk_w_q.astype(BF16)
    keys = pk_keys.reshape(DEPTH, 2 * PEER_HEADS, N_KEYS, PEER_DQH).astype(BF16)
    u_all, v_all = _pack_table(pk_u), _pack_table(pk_v)
    nb = bsz // BATCH_SPLIT
    t = nb * seq
    outs = []
    for part in range(BATCH_SPLIT):
        lo = part * nb
        x2 = x[lo:lo + nb].reshape(t, d)
        for i in range(DEPTH):
            sh1, sc1, g1, sh2, sc2, g2 = (
                mod[i, lo:lo + nb, n * d:(n + 1) * d].reshape(nb, 1, d) for n in range(ADA_CHUNKS))
            j = i // 2
            if i % 2 == 0:
                u2 = _norm_glu(x2, seq, norm_mix_g[i], sc1, sh1, w_in[j], cv_b_in[j])
                x2 = _conv_block(u2.reshape(nb, seq, d), x2.reshape(nb, seq, d), g1,
                                 cv_w_dw[j], cv_b_dw[j], cv_ln_g[j], cv_ln_b[j],
                                 w_out[j], cv_b_out[j]).reshape(t, d)
            else:
                qkv = _norm_mm(x2, seq, norm_mix_g[i], sc1, sh1, w_qkv[j], BF16)
                o = _attention(qkv.reshape(nb, seq, 3 * d))
                x2 = _mm_res(o.reshape(t, d), w_o[j], x2, g1, seq)
            y2 = _peer(x2, seq, norm_ffn_g[i], sc2, sh2, w_q[i], keys[i], u_all, v_all, i)
            x2 = _residual(x2, y2, g2, seq, final_g if i == DEPTH - 1 else None)
        outs.append(x2.reshape(nb, seq, d))
    return jnp.concatenate(outs, axis=0)
```

```python
import functools
import math

import jax
import jax.numpy as jnp
from jax import lax
from jax.experimental import pallas as pl
from jax.experimental.pallas import tpu as pltpu
from jax.experimental.pallas import tpu_sc as plsc

D_MODEL = 1024
DEPTH = 2
CONV_WIDTH = 31
N_HEADS = 16
HEAD_DIM = 64
PEER_HEADS = 8
PEER_DQH = 128
N_KEYS = 128
PEER_TOPK = 16
PEER_HK = PEER_HEADS * PEER_TOPK
ADA_CHUNKS = 6
EPS = 1e-6

LANES = 128
SUBLANES = 8
SC_LANES = 16
VMEM_LIMIT = 48 * 1024 * 1024

F32 = jnp.float32
BF16 = jnp.bfloat16
NEG_INF = float("-inf")


def _cparams(*sem):
    return pltpu.CompilerParams(dimension_semantics=sem, vmem_limit_bytes=VMEM_LIMIT)


def _norm_mod(x, gn, sc, sh):
    ms = jnp.mean(x * x, axis=-1, keepdims=True)
    return (x * lax.rsqrt(ms + EPS)) * gn * (1.0 + sc) + sh


def _ada_kernel(c_ref, w_ref, b_ref, o_ref):
    c = c_ref[...]
    ca = c * jax.nn.sigmoid(c)
    o_ref[0] = jnp.dot(ca.astype(BF16), w_ref[0].astype(BF16),
                       preferred_element_type=F32) + b_ref[0]


def _ada_mod(c, ada_w, ada_b):
    depth, d, n = ada_w.shape
    bsz = c.shape[0]
    tn = 512
    return pl.pallas_call(
        _ada_kernel,
        grid=(depth, n // tn),
        in_specs=[pl.BlockSpec((bsz, d), lambda i, j: (0, 0)),
                  pl.BlockSpec((1, d, tn), lambda i, j: (i, 0, j)),
                  pl.BlockSpec((1, 1, tn), lambda i, j: (i, 0, j))],
        out_specs=pl.BlockSpec((1, bsz, tn), lambda i, j: (i, 0, j)),
        out_shape=jax.ShapeDtypeStruct((depth, bsz, n), F32),
        compiler_params=_cparams("parallel", "parallel"),
        name="ada_mod",
    )(c, ada_w, ada_b.reshape(depth, 1, n))


def _norm_mm_kernel(x_ref, gn_ref, sc_ref, sh_ref, w_ref, o_ref):
    h = _norm_mod(x_ref[...], gn_ref[...], sc_ref[0], sh_ref[0])
    o_ref[...] = jnp.dot(h.astype(BF16), w_ref[...],
                         preferred_element_type=F32).astype(o_ref.dtype)


def _norm_mm(x2, seq, gn, sc, sh, w_bf16, out_dtype, tm=256):
    t, d = x2.shape
    n = w_bf16.shape[1]
    per_seq = lambda i: ((i * tm) // seq, 0, 0)
    return pl.pallas_call(
        _norm_mm_kernel,
        grid=(t // tm,),
        in_specs=[pl.BlockSpec((tm, d), lambda i: (i, 0)),
                  pl.BlockSpec((1, d), lambda i: (0, 0)),
                  pl.BlockSpec((1, 1, d), per_seq),
                  pl.BlockSpec((1, 1, d), per_seq),
                  pl.BlockSpec((d, n), lambda i: (0, 0))],
        out_specs=pl.BlockSpec((tm, n), lambda i: (i, 0)),
        out_shape=jax.ShapeDtypeStruct((t, n), out_dtype),
        compiler_params=_cparams("parallel"),
        name="norm_mm",
    )(x2, gn.reshape(1, d), sc, sh, w_bf16)


def _norm_glu_kernel(x_ref, gn_ref, sc_ref, sh_ref, wa_ref, wg_ref, ba_ref, bg_ref, o_ref):
    h = _norm_mod(x_ref[...], gn_ref[...], sc_ref[0], sh_ref[0]).astype(BF16)
    a = jnp.dot(h, wa_ref[...], preferred_element_type=F32) + ba_ref[...]
    g = jnp.dot(h, wg_ref[...], preferred_element_type=F32) + bg_ref[...]
    o_ref[...] = a * jax.nn.sigmoid(g)


def _norm_glu(x2, seq, gn, sc, sh, w_in_bf16, b_in, tm=256):
    t, d = x2.shape
    n = w_in_bf16.shape[1] // 2
    per_seq = lambda i: ((i * tm) // seq, 0, 0)
    b2 = b_in.reshape(1, 2 * n)
    return pl.pallas_call(
        _norm_glu_kernel,
        grid=(t // tm,),
        in_specs=[pl.BlockSpec((tm, d), lambda i: (i, 0)),
                  pl.BlockSpec((1, d), lambda i: (0, 0)),
                  pl.BlockSpec((1, 1, d), per_seq),
                  pl.BlockSpec((1, 1, d), per_seq),
                  pl.BlockSpec((d, n), lambda i: (0, 0)),
                  pl.BlockSpec((d, n), lambda i: (0, 1)),
                  pl.BlockSpec((1, n), lambda i: (0, 0)),
                  pl.BlockSpec((1, n), lambda i: (0, 1))],
        out_specs=pl.BlockSpec((tm, n), lambda i: (i, 0)),
        out_shape=jax.ShapeDtypeStruct((t, n), F32),
        compiler_params=_cparams("parallel"),
        name="norm_glu",
    )(x2, gn.reshape(1, d), sc, sh, w_in_bf16, w_in_bf16, b2, b2)


HALO = 32


def _conv_kernel(cur_ref, prev_ref, wdw_ref, bdw_ref, lng_ref, lnb_ref, wout_ref,
                 bout_ref, x_ref, g_ref, o_ref, win_s):
    ts = cur_ref.shape[1]
    first = pl.program_id(1) == 0
    win_s[0:HALO, :] = jnp.where(first, 0.0, prev_ref[0])
    win_s[HALO:HALO + ts, :] = cur_ref[0]
    acc = jnp.zeros(cur_ref.shape[1:], F32) + bdw_ref[...]
    off = HALO - (CONV_WIDTH - 1)
    for k in range(CONV_WIDTH):
        acc = acc + win_s[off + k:off + k + ts, :] * wdw_ref[k:k + 1, :]
    mu = jnp.mean(acc, axis=-1, keepdims=True)
    cen = acc - mu
    var = jnp.mean(cen * cen, axis=-1, keepdims=True)
    y = cen * lax.rsqrt(var + EPS) * lng_ref[...] + lnb_ref[...]
    y = y * jax.nn.sigmoid(y)
    m = jnp.dot(y.astype(BF16), wout_ref[...], preferred_element_type=F32) + bout_ref[...]
    o_ref[0] = x_ref[0] + g_ref[0] * m


def _conv_block(u3, x3, g1, w_dw, b_dw, ln_g, ln_b, w_out_bf16, b_out, ts=256):
    bsz, seq, d = u3.shape
    r = ts // HALO
    wdw = jnp.concatenate([w_dw, jnp.zeros((HALO - CONV_WIDTH, d), F32)], axis=0)
    vec = lambda b, i: (0, 0)
    return pl.pallas_call(
        _conv_kernel,
        grid=(bsz, seq // ts),
        in_specs=[pl.BlockSpec((1, ts, d), lambda b, i: (b, i, 0)),
                  pl.BlockSpec((1, HALO, d), lambda b, i: (b, jnp.maximum(i * r - 1, 0), 0)),
                  pl.BlockSpec((HALO, d), vec),
                  pl.BlockSpec((1, d), vec),
                  pl.BlockSpec((1, d), vec),
                  pl.BlockSpec((1, d), vec),
                  pl.BlockSpec((d, d), vec),
                  pl.BlockSpec((1, d), vec),
                  pl.BlockSpec((1, ts, d), lambda b, i: (b, i, 0)),
                  pl.BlockSpec((1, 1, d), lambda b, i: (b, 0, 0))],
        out_specs=pl.BlockSpec((1, ts, d), lambda b, i: (b, i, 0)),
        out_shape=jax.ShapeDtypeStruct((bsz, seq, d), F32),
        scratch_shapes=[pltpu.VMEM((HALO + ts, d), F32)],
        compiler_params=_cparams("parallel", "arbitrary"),
        name="conv_block",
    )(u3, u3, wdw, b_dw.reshape(1, d), ln_g.reshape(1, d), ln_b.reshape(1, d),
      w_out_bf16, b_out.reshape(1, d), x3, g1)


ATT_T = 256


def _attn_kernel(q_ref, k_ref, v_ref, o_ref):
    t = ATT_T
    qi = pl.program_id(2)
    q2 = q_ref[0] * (1.0 / math.sqrt(HEAD_DIM))
    lane = lax.broadcasted_iota(jnp.int32, (1, LANES), 1)
    head_mask = (lane < HEAD_DIM, lane >= HEAD_DIM)
    row = lax.broadcasted_iota(jnp.int32, (t, t), 0)
    col = lax.broadcasted_iota(jnp.int32, (t, t), 1)
    tri = (row >= col).astype(BF16)
    tri2 = jnp.concatenate([tri, tri], axis=0)
    qh = tuple(jnp.where(m, q2, jnp.zeros_like(q2)) for m in head_mask)
    nt = (((1,), (1,)), ((), ()))

    def tile(kt, suf, acc, diag):
        start = pl.multiple_of(kt * t, t)
        k2 = k_ref[0, pl.ds(start, t), :]
        v2 = v_ref[0, pl.ds(start, t), :]
        new_suf = []
        for h in range(2):
            z = lax.dot_general(qh[h], k2, nt, preferred_element_type=F32)
            sp = jnp.maximum(z, 0.0) + jnp.log(1.0 + jnp.exp(-jnp.abs(z)))
            if diag:
                sp = jnp.where(col < row, sp, 0.0)
            hi = sp.astype(BF16)
            mid = (sp - hi.astype(F32)).astype(BF16)
            cs = jnp.dot(jnp.concatenate([hi, mid], axis=1), tri2,
                         preferred_element_type=F32)
            a = jnp.exp(z - cs - suf[h])
            if diag:
                a = jnp.where(col < row, a, 0.0)
            pv = jnp.dot(a.astype(BF16), v2, preferred_element_type=F32)
            acc = acc + jnp.where(head_mask[h], pv, 0.0)
            new_suf.append(suf[h] + cs[:, 0:1])
        return tuple(new_suf), acc

    zero = jnp.zeros((t, 1), F32)
    suf, acc = tile(qi, (zero, zero), jnp.zeros((t, LANES), F32), True)

    def body(n, carry):
        s0, s1, acc = carry
        (s0, s1), acc = tile(qi - n, (s0, s1), acc, False)
        return s0, s1, acc

    _, _, acc = lax.fori_loop(1, qi + 1, body, (suf[0], suf[1], acc))
    o_ref[0] = acc


def _attention(qkv3):
    bsz, seq, n3 = qkv3.shape
    d = n3 // 3
    nb = d // LANES
    return pl.pallas_call(
        _attn_kernel,
        grid=(bsz, nb, seq // ATT_T),
        in_specs=[pl.BlockSpec((1, ATT_T, LANES), lambda b, h, i: (b, i, h)),
                  pl.BlockSpec((1, seq, LANES), lambda b, h, i: (b, 0, nb + h)),
                  pl.BlockSpec((1, seq, LANES), lambda b, h, i: (b, 0, 2 * nb + h))],
        out_specs=pl.BlockSpec((1, ATT_T, LANES), lambda b, h, i: (b, i, h)),
        out_shape=jax.ShapeDtypeStruct((bsz, seq, d), F32),
        compiler_params=_cparams("parallel", "parallel", "arbitrary"),
        name="sb_attention",
    )(qkv3, qkv3, qkv3)


def _mm_res_kernel(a_ref, w_ref, x_ref, g_ref, o_ref):
    m = jnp.dot(a_ref[...].astype(BF16), w_ref[...], preferred_element_type=F32)
    o_ref[...] = x_ref[...] + g_ref[0] * m


def _mm_res(a2, w_bf16, x2, g, seq, tm=512):
    t, d = x2.shape
    k = a2.shape[1]
    return pl.pallas_call(
        _mm_res_kernel,
        grid=(t // tm,),
        in_specs=[pl.BlockSpec((tm, k), lambda i: (i, 0)),
                  pl.BlockSpec((k, d), lambda i: (0, 0)),
                  pl.BlockSpec((tm, d), lambda i: (i, 0)),
                  pl.BlockSpec((1, 1, d), lambda i: ((i * tm) // seq, 0, 0))],
        out_specs=pl.BlockSpec((tm, d), lambda i: (i, 0)),
        out_shape=jax.ShapeDtypeStruct((t, d), F32),
        compiler_params=_cparams("parallel"),
        name="mm_res",
    )(a2, w_bf16, x2, g)


PAIR_ROWS = tuple(PEER_TOPK // (i + 1) for i in range(PEER_TOPK))


def _extract_top(s, payload, k):
    rows = lax.broadcasted_iota(jnp.int32, s.shape, 0)
    big = s.shape[0]
    vals, outs = [], []
    for _ in range(k):
        m = jnp.max(s, axis=0, keepdims=True)
        r = jnp.min(jnp.where(s == m, rows, big), axis=0, keepdims=True)
        sel = rows == r
        vals.append(m)
        if payload is None:
            outs.append(r)
        else:
            outs.append(jnp.max(jnp.where(sel, payload, -1), axis=0, keepdims=True))
        s = jnp.where(sel, NEG_INF, s)
    return jnp.concatenate(vals, axis=0), jnp.concatenate(outs, axis=0)


def _pair_candidates(a, ia, b, ib):
    sub = lax.broadcasted_iota(jnp.int32, (SUBLANES, a.shape[1]), 0)
    cand, ids = [], []
    for i in range(SUBLANES):
        for j0 in range(0, PAIR_ROWS[i], SUBLANES):
            n = min(PAIR_ROWS[i] - j0, SUBLANES)
            c = a[i:i + 1, :] + b[j0:j0 + SUBLANES, :]
            e = ia[i:i + 1, :] * N_KEYS + ib[j0:j0 + SUBLANES, :]
            if n < SUBLANES:
                c = jnp.where(sub < n, c, NEG_INF)
            cand.append(c)
            ids.append(e)
    cand.append(a[SUBLANES:, :] + b[0:1, :])
    ids.append(ia[SUBLANES:, :] * N_KEYS + ib[0:1, :])
    return jnp.concatenate(cand, axis=0), jnp.concatenate(ids, axis=0)


def _peer_route_kernel(row_base, x_ref, gn_ref, sc_ref, sh_ref, wq_ref, keys_ref,
                       h_ref, e_ref, g_ref, q_s, e_s, g_s):
    h = _norm_mod(x_ref[...], gn_ref[...], sc_ref[0], sh_ref[0])
    h_ref[...] = h
    q = jnp.dot(h.astype(BF16), wq_ref[...], preferred_element_type=F32)
    for j in range(2 * PEER_HEADS):
        q_s[j] = q[:, j * PEER_DQH:(j + 1) * PEER_DQH].astype(BF16)
    nt = (((1,), (1,)), ((), ()))

    def head(hd, carry):
        tops = []
        for c in range(2):
            s_t = lax.dot_general(keys_ref[2 * hd + c], q_s[2 * hd + c], nt,
                                  preferred_element_type=F32)
            tops.append(_extract_top(s_t, None, PEER_TOPK))
        cand, ids = _pair_candidates(tops[0][0], tops[0][1], tops[1][0], tops[1][1])
        best, experts = _extract_top(cand, ids, PEER_TOPK)
        ex = jnp.exp(best - best[0:1, :])
        g_s[hd] = ex / jnp.sum(ex, axis=0, keepdims=True)
        e_s[hd] = experts
        return carry

    lax.fori_loop(0, PEER_HEADS, head, 0)
    tm = x_ref.shape[0]
    e_ref[...] = e_s[...].reshape(PEER_HK, tm).T + row_base
    g_ref[...] = g_s[...].reshape(PEER_HK, tm).T


def _peer_route(x2, seq, gn, sc, sh, wq_bf16, keys_bf16, row_base, tm=256):
    t, d = x2.shape
    nq = wq_bf16.shape[1]
    per_seq = lambda i: ((i * tm) // seq, 0, 0)
    return pl.pallas_call(
        functools.partial(_peer_route_kernel, row_base),
        grid=(t // tm,),
        in_specs=[pl.BlockSpec((tm, d), lambda i: (i, 0)),
                  pl.BlockSpec((1, d), lambda i: (0, 0)),
                  pl.BlockSpec((1, 1, d), per_seq),
                  pl.BlockSpec((1, 1, d), per_seq),
                  pl.BlockSpec((d, nq), lambda i: (0, 0)),
                  pl.BlockSpec((2 * PEER_HEADS, N_KEYS, PEER_DQH), lambda i: (0, 0, 0))],
        out_specs=[pl.BlockSpec((tm, d), lambda i: (i, 0)),
                   pl.BlockSpec((tm, PEER_HK), lambda i: (i, 0)),
                   pl.BlockSpec((tm, PEER_HK), lambda i: (i, 0))],
        out_shape=[jax.ShapeDtypeStruct((t, d), F32),
                   jax.ShapeDtypeStruct((t, PEER_HK), jnp.int32),
                   jax.ShapeDtypeStruct((t, PEER_HK), F32)],
        scratch_shapes=[pltpu.VMEM((2 * PEER_HEADS, tm, PEER_DQH), BF16),
                        pltpu.VMEM((PEER_HEADS, PEER_TOPK, tm), jnp.int32),
                        pltpu.VMEM((PEER_HEADS, PEER_TOPK, tm), F32)],
        compiler_params=_cparams("parallel"),
        name="peer_route",
    )(x2, gn.reshape(1, d), sc, sh, wq_bf16, keys_bf16)


SC_ROWS = 16
SC_GROUPS = PEER_HK // SC_ROWS
SC_TB = 32
SC_CHUNKS = D_MODEL // SC_LANES
SC_HALF = D_MODEL // 2
SC_WCH = SC_HALF // SC_LANES
SC_NBUF = 4


def _pack_table(tab):
    b = lax.bitcast_convert_type(tab.reshape(-1, D_MODEL).astype(BF16), jnp.uint16)
    b = b.astype(jnp.uint32)
    return lax.bitcast_convert_type(b[:, :SC_HALF] | (b[:, SC_HALF:] << 16), jnp.int32)


def _unpack_pair(w):
    lo = plsc.bitcast(lax.shift_left(w, jnp.int32(16)), F32)
    hi = plsc.bitcast(w & jnp.int32(-65536), F32)
    return lo, hi


def _sc_mesh():
    return plsc.VectorSubcoreMesh(core_axis_name="c", subcore_axis_name="s")


def _sc_pipeline(n_items, gather, compute):
    ahead = SC_NBUF - 1
    for i in range(ahead):
        gather(i, i).start()

    @pl.loop(0, n_items, step=SC_NBUF)
    def _(it):
        for b in range(SC_NBUF):
            i = it + b
            gather(i, b).wait()

            @pl.when(i + ahead < n_items)
            def _():
                gather(i + ahead, (b + ahead) % SC_NBUF).start()

            compute(i, b)


def _tree_sum(terms):
    while len(terms) > 1:
        terms = [terms[i] + terms[i + 1] for i in range(0, len(terms), 2)]
    return terms[0]


def _peer_pre(h2, idx, u_tab):
    t, d = h2.shape
    info = plsc.get_sparse_core_info()
    nc, nw = info.num_cores, info.num_cores * info.num_subcores
    tpw = t // nw
    n_items = SC_TB * SC_GROUPS

    @functools.partial(
        pl.kernel, mesh=_sc_mesh(),
        out_type=jax.ShapeDtypeStruct((t, PEER_HK), F32),
        scratch_types=[pltpu.VMEM((SC_TB, PEER_HK), jnp.int32),
                       pltpu.VMEM((SC_TB, d), F32),
                       pltpu.VMEM((SC_TB, PEER_HK), F32),
                       pltpu.VMEM((SC_ROWS, SC_LANES), F32)]
                      + [pltpu.VMEM((SC_ROWS, SC_HALF), jnp.int32)] * SC_NBUF
                      + [pltpu.SemaphoreType.DMA] * SC_NBUF,
        compiler_params=pltpu.CompilerParams(needs_layout_passes=False),
        name="peer_pre")
    def k(h_hbm, idx_hbm, u_hbm, p_hbm, idx_v, h_v, p_v, tr_v, *ring):
        wid = lax.axis_index("s") * nc + lax.axis_index("c")
        bufs, sems = ring[:SC_NBUF], ring[SC_NBUF:]
        lane = lax.iota(jnp.int32, SC_LANES)

        def gather(it, b):
            tl, g = it // SC_GROUPS, it % SC_GROUPS
            return pltpu.make_async_copy(
                u_hbm.at[idx_v.at[tl, pl.ds(g * SC_ROWS, SC_ROWS)]], bufs[b], sems[b])

        def compute(it, b):
            tl, g = it // SC_GROUPS, it % SC_GROUPS
            buf = bufs[b]

            def body(c, accs):
                sl = pl.ds(c * SC_LANES, SC_LANES)
                x_lo = h_v[tl, sl]
                x_hi = h_v[tl, pl.ds(SC_HALF + c * SC_LANES, SC_LANES)]
                out = []
                for r in range(SC_ROWS):
                    lo, hi = _unpack_pair(buf[r, sl])
                    out.append(accs[r] + (lo * x_lo + hi * x_hi))
                return tuple(out)

            zero = jnp.zeros((SC_LANES,), F32)
            accs = lax.fori_loop(0, SC_WCH, body, (zero,) * SC_ROWS)
            skew = [(lane + j) & (SC_LANES - 1) for j in range(SC_LANES)]
            for r in range(SC_ROWS):
                plsc.store_scatter(tr_v, [jnp.full((SC_LANES,), r, jnp.int32), skew[r]], accs[r])
            cols = [plsc.load_gather(tr_v, [lane, skew[j]]) for j in range(SC_LANES)]
            p_v[tl, pl.ds(g * SC_ROWS, SC_ROWS)] = _tree_sum(cols)

        @pl.loop(0, tpw // SC_TB)
        def _(blk):
            t0 = wid * tpw + blk * SC_TB
            pltpu.sync_copy(idx_hbm.at[pl.ds(t0, SC_TB)], idx_v)
            pltpu.sync_copy(h_hbm.at[pl.ds(t0, SC_TB)], h_v)
            _sc_pipeline(n_items, gather, compute)
            pltpu.sync_copy(p_v, p_hbm.at[pl.ds(t0, SC_TB)])

    return k(h2, idx, u_tab)


def _peer_post(act, idx, v_tab):
    t = act.shape[0]
    d = 2 * v_tab.shape[1]
    info = plsc.get_sparse_core_info()
    nc, nw = info.num_cores, info.num_cores * info.num_subcores
    tpw = t // nw
    n_items = SC_TB * SC_GROUPS

    @functools.partial(
        pl.kernel, mesh=_sc_mesh(),
        out_type=jax.ShapeDtypeStruct((t, d), F32),
        scratch_types=[pltpu.VMEM((SC_TB, PEER_HK), jnp.int32),
                       pltpu.VMEM((SC_TB, PEER_HK), F32),
                       pltpu.VMEM((SC_TB, d), F32)]
                      + [pltpu.VMEM((SC_ROWS, SC_HALF), jnp.int32)] * SC_NBUF
                      + [pltpu.SemaphoreType.DMA] * SC_NBUF,
        compiler_params=pltpu.CompilerParams(needs_layout_passes=False),
        name="peer_post")
    def k(a_hbm, idx_hbm, v_hbm, y_hbm, idx_v, a_v, y_v, *ring):
        wid = lax.axis_index("s") * nc + lax.axis_index("c")
        bufs, sems = ring[:SC_NBUF], ring[SC_NBUF:]

        def gather(it, b):
            tl, g = it // SC_GROUPS, it % SC_GROUPS
            return pltpu.make_async_copy(
                v_hbm.at[idx_v.at[tl, pl.ds(g * SC_ROWS, SC_ROWS)]], bufs[b], sems[b])

        def compute(it, b):
            tl, g = it // SC_GROUPS, it % SC_GROUPS
            buf = bufs[b]
            tlv = jnp.full((SC_LANES,), tl, jnp.int32)
            w = tuple(plsc.load_gather(a_v, [tlv, jnp.full((SC_LANES,), g * SC_ROWS + r, jnp.int32)])
                      for r in range(SC_ROWS))

            @plsc.parallel_loop(0, SC_WCH, unroll=2)
            def _(c):
                sl = pl.ds(c * SC_LANES, SC_LANES)
                parts = [_unpack_pair(buf[r, sl]) for r in range(SC_ROWS)]
                plsc.addupdate(y_v.at[tl, sl],
                               _tree_sum([w[r] * parts[r][0] for r in range(SC_ROWS)]))
                plsc.addupdate(y_v.at[tl, pl.ds(SC_HALF + c * SC_LANES, SC_LANES)],
                               _tree_sum([w[r] * parts[r][1] for r in range(SC_ROWS)]))

        @pl.loop(0, tpw // SC_TB)
        def _(blk):
            t0 = wid * tpw + blk * SC_TB
            pltpu.sync_copy(idx_hbm.at[pl.ds(t0, SC_TB)], idx_v)
            pltpu.sync_copy(a_hbm.at[pl.ds(t0, SC_TB)], a_v)

            @pl.loop(0, SC_TB)
            def _(r):
                @plsc.parallel_loop(0, SC_CHUNKS, unroll=4)
                def _(c):
                    y_v[r, pl.ds(c * SC_LANES, SC_LANES)] = jnp.zeros((SC_LANES,), F32)

            _sc_pipeline(n_items, gather, compute)
            pltpu.sync_copy(y_v, y_hbm.at[pl.ds(t0, SC_TB)])

    return k(act, idx, v_tab)


def _act_kernel(p_ref, g_ref, o_ref):
    p = p_ref[...]
    o_ref[...] = (p * (lax.erf(p * (1.0 / math.sqrt(2.0))) + 1.0) * 0.5) * g_ref[...]


def _expert_act(p, gates, tm=2048):
    t, n = p.shape
    spec = pl.BlockSpec((tm, n), lambda i: (i, 0))
    return pl.pallas_call(
        _act_kernel, grid=(t // tm,), in_specs=[spec, spec], out_specs=spec,
        out_shape=jax.ShapeDtypeStruct((t, n), F32),
        compiler_params=_cparams("parallel"), name="expert_act",
    )(p, gates)


def _res_kernel(x_ref, y_ref, g_ref, o_ref):
    o_ref[...] = x_ref[...] + g_ref[0] * y_ref[...]


def _res_norm_kernel(x_ref, y_ref, g_ref, fg_ref, o_ref):
    x = x_ref[...] + g_ref[0] * y_ref[...]
    ms = jnp.mean(x * x, axis=-1, keepdims=True)
    o_ref[...] = x * lax.rsqrt(ms + EPS) * fg_ref[...]


def _residual(x2, y2, g, seq, final_g=None, tm=512):
    t, d = x2.shape
    tile = pl.BlockSpec((tm, d), lambda i: (i, 0))
    gspec = pl.BlockSpec((1, 1, d), lambda i: ((i * tm) // seq, 0, 0))
    if final_g is None:
        kern, ins, args = _res_kernel, [tile, tile, gspec], (x2, y2, g)
    else:
        kern = _res_norm_kernel
        ins = [tile, tile, gspec, pl.BlockSpec((1, d), lambda i: (0, 0))]
        args = (x2, y2, g, final_g.reshape(1, d))
    return pl.pallas_call(
        kern, grid=(t // tm,), in_specs=ins, out_specs=tile,
        out_shape=jax.ShapeDtypeStruct((t, d), F32),
        compiler_params=_cparams("parallel"), name="residual",
    )(*args)


def _peer(x2, seq, gn, sc, sh, wq_bf16, keys_bf16, u_all, v_all, layer):
    rows = _peer_route(x2, seq, gn, sc, sh, wq_bf16, keys_bf16, layer * N_KEYS * N_KEYS)
    h2, experts, gates = rows
    p = _peer_pre(h2, experts, u_all)
    act = _expert_act(p, gates)
    return _peer_post(act, experts, v_all)


BATCH_SPLIT = 4


def kernel(x, c, ada_w, ada_b, norm_mix_g, norm_ffn_g, cv_w_in, cv_b_in, cv_w_dw, cv_b_dw, cv_ln_g, cv_ln_b, cv_w_out, cv_b_out, sb_w_qkv, sb_w_o, pk_w_q, pk_keys, pk_u, pk_v, final_g):
    bsz, seq, d = x.shape
    mod = _ada_mod(c, ada_w, ada_b)
    w_in, w_out = cv_w_in.astype(BF16), cv_w_out.astype(BF16)
    w_qkv, w_o = sb_w_qkv.astype(BF16), sb_w_o.astype(BF16)
    w_q = pk_w_q.astype(BF16)
    keys = pk_keys.reshape(DEPTH, 2 * PEER_HEADS, N_KEYS, PEER_DQH).astype(BF16)
    u_all, v_all = _pack_table(pk_u), _pack_table(pk_v)
    nb = bsz // BATCH_SPLIT
    t = nb * seq
    outs = []
    for part in range(BATCH_SPLIT):
        lo = part * nb
        x2 = x[lo:lo + nb].reshape(t, d)
        for i in range(DEPTH):
            sh1, sc1, g1, sh2, sc2, g2 = (
                mod[i, lo:lo + nb, n * d:(n + 1) * d].reshape(nb, 1, d) for n in range(ADA_CHUNKS))
            j = i // 2
            if i % 2 == 0:
                u2 = _norm_glu(x2, seq, norm_mix_g[i], sc1, sh1, w_in[j], cv_b_in[j])
                x2 = _conv_block(u2.reshape(nb, seq, d), x2.reshape(nb, seq, d), g1,
                                 cv_w_dw[j], cv_b_dw[j], cv_ln_g[j], cv_ln_b[j],
                                 w_out[j], cv_b_out[j]).reshape(t, d)
            else:
                qkv = _norm_mm(x2, seq, norm_mix_g[i], sc1, sh1, w_qkv[j], BF16)
                o = _attention(qkv.reshape(nb, seq, 3 * d))
                x2 = _mm_res(o.reshape(t, d), w_o[j], x2, g1, seq)
            y2 = _peer(x2, seq, norm_ffn_g[i], sc2, sh2, w_q[i], keys[i], u_all, v_all, i)
            x2 = _residual(x2, y2, g2, seq, final_g if i == DEPTH - 1 else None)
        outs.append(x2.reshape(nb, seq, d))
    return jnp.concatenate(outs, axis=0)
```

```python
import functools
import math

import jax
import jax.numpy as jnp
from jax import lax
from jax.experimental import pallas as pl
from jax.experimental.pallas import tpu as pltpu
from jax.experimental.pallas import tpu_sc as plsc

D_MODEL = 1024
DEPTH = 2
CONV_WIDTH = 31
N_HEADS = 16
HEAD_DIM = 64
PEER_HEADS = 8
PEER_DQH = 128
N_KEYS = 128
PEER_TOPK = 16
PEER_HK = PEER_HEADS * PEER_TOPK
ADA_CHUNKS = 6
EPS = 1e-6

LANES = 128
SUBLANES = 8
SC_LANES = 16
VMEM_LIMIT = 48 * 1024 * 1024

F32 = jnp.float32
BF16 = jnp.bfloat16
NEG_INF = float("-inf")


def _cparams(*sem):
    return pltpu.CompilerParams(dimension_semantics=sem, vmem_limit_bytes=VMEM_LIMIT)


def _norm_mod(x, gn, sc, sh):
    ms = jnp.mean(x * x, axis=-1, keepdims=True)
    return (x * lax.rsqrt(ms + EPS)) * gn * (1.0 + sc) + sh


def _ada_kernel(c_ref, w_ref, b_ref, o_ref):
    c = c_ref[...]
    ca = c * jax.nn.sigmoid(c)
    o_ref[0] = jnp.dot(ca.astype(BF16), w_ref[0].astype(BF16),
                       preferred_element_type=F32) + b_ref[0]


def _ada_mod(c, ada_w, ada_b):
    depth, d, n = ada_w.shape
    bsz = c.shape[0]
    tn = 512
    return pl.pallas_call(
        _ada_kernel,
        grid=(depth, n // tn),
        in_specs=[pl.BlockSpec((bsz, d), lambda i, j: (0, 0)),
                  pl.BlockSpec((1, d, tn), lambda i, j: (i, 0, j)),
                  pl.BlockSpec((1, 1, tn), lambda i, j: (i, 0, j))],
        out_specs=pl.BlockSpec((1, bsz, tn), lambda i, j: (i, 0, j)),
        out_shape=jax.ShapeDtypeStruct((depth, bsz, n), F32),
        compiler_params=_cparams("parallel", "parallel"),
        name="ada_mod",
    )(c, ada_w, ada_b.reshape(depth, 1, n))


def _norm_mm_kernel(x_ref, gn_ref, sc_ref, sh_ref, w_ref, o_ref):
    h = _norm_mod(x_ref[...], gn_ref[...], sc_ref[0], sh_ref[0])
    o_ref[...] = jnp.dot(h.astype(BF16), w_ref[...],
                         preferred_element_type=F32).astype(o_ref.dtype)


def _norm_mm(x2, seq, gn, sc, sh, w_bf16, out_dtype, tm=256):
    t, d = x2.shape
    n = w_bf16.shape[1]
    per_seq = lambda i: ((i * tm) // seq, 0, 0)
    return pl.pallas_call(
        _norm_mm_kernel,
        grid=(t // tm,),
        in_specs=[pl.BlockSpec((tm, d), lambda i: (i, 0)),
                  pl.BlockSpec((1, d), lambda i: (0, 0)),
                  pl.BlockSpec((1, 1, d), per_seq),
                  pl.BlockSpec((1, 1, d), per_seq),
                  pl.BlockSpec((d, n), lambda i: (0, 0))],
        out_specs=pl.BlockSpec((tm, n), lambda i: (i, 0)),
        out_shape=jax.ShapeDtypeStruct((t, n), out_dtype),
        compiler_params=_cparams("parallel"),
        name="norm_mm",
    )(x2, gn.reshape(1, d), sc, sh, w_bf16)


def _norm_glu_kernel(x_ref, gn_ref, sc_ref, sh_ref, wa_ref, wg_ref, ba_ref, bg_ref, o_ref):
    h = _norm_mod(x_ref[...], gn_ref[...], sc_ref[0], sh_ref[0]).astype(BF16)
    a = jnp.dot(h, wa_ref[...], preferred_element_type=F32) + ba_ref[...]
    g = jnp.dot(h, wg_ref[...], preferred_element_type=F32) + bg_ref[...]
    o_ref[...] = a * jax.nn.sigmoid(g)


def _norm_glu(x2, seq, gn, sc, sh, w_in_bf16, b_in, tm=256):
    t, d = x2.shape
    n = w_in_bf16.shape[1] // 2
    per_seq = lambda i: ((i * tm) // seq, 0, 0)
    b2 = b_in.reshape(1, 2 * n)
    return pl.pallas_call(
        _norm_glu_kernel,
        grid=(t // tm,),
        in_specs=[pl.BlockSpec((tm, d), lambda i: (i, 0)),
                  pl.BlockSpec((1, d), lambda i: (0, 0)),
                  pl.BlockSpec((1, 1, d), per_seq),
                  pl.BlockSpec((1, 1, d), per_seq),
                  pl.BlockSpec((d, n), lambda i: (0, 0)),
                  pl.BlockSpec((d, n), lambda i: (0, 1)),
                  pl.BlockSpec((1, n), lambda i: (0, 0)),
                  pl.BlockSpec((1, n), lambda i: (0, 1))],
        out_specs=pl.BlockSpec((tm, n), lambda i: (i, 0)),
        out_shape=jax.ShapeDtypeStruct((t, n), F32),
        compiler_params=_cparams("parallel"),
        name="norm_glu",
    )(x2, gn.reshape(1, d), sc, sh, w_in_bf16, w_in_bf16, b2, b2)


HALO = 32


def _conv_kernel(cur_ref, prev_ref, wdw_ref, bdw_ref, lng_ref, lnb_ref, wout_ref,
                 bout_ref, x_ref, g_ref, o_ref, win_s):
    ts = cur_ref.shape[1]
    first = pl.program_id(1) == 0
    win_s[0:HALO, :] = jnp.where(first, 0.0, prev_ref[0])
    win_s[HALO:HALO + ts, :] = cur_ref[0]
    acc = jnp.zeros(cur_ref.shape[1:], F32) + bdw_ref[...]
    off = HALO - (CONV_WIDTH - 1)
    for k in range(CONV_WIDTH):
        acc = acc + win_s[off + k:off + k + ts, :] * wdw_ref[k:k + 1, :]
    mu = jnp.mean(acc, axis=-1, keepdims=True)
    cen = acc - mu
    var = jnp.mean(cen * cen, axis=-1, keepdims=True)
    y = cen * lax.rsqrt(var + EPS) * lng_ref[...] + lnb_ref[...]
    y = y * jax.nn.sigmoid(y)
    m = jnp.dot(y.astype(BF16), wout_ref[...], preferred_element_type=F32) + bout_ref[...]
    o_ref[0] = x_ref[0] + g_ref[0] * m


def _conv_block(u3, x3, g1, w_dw, b_dw, ln_g, ln_b, w_out_bf16, b_out, ts=256):
    bsz, seq, d = u3.shape
    r = ts // HALO
    wdw = jnp.concatenate([w_dw, jnp.zeros((HALO - CONV_WIDTH, d), F32)], axis=0)
    vec = lambda b, i: (0, 0)
    return pl.pallas_call(
        _conv_kernel,
        grid=(bsz, seq // ts),
        in_specs=[pl.BlockSpec((1, ts, d), lambda b, i: (b, i, 0)),
                  pl.BlockSpec((1, HALO, d), lambda b, i: (b, jnp.maximum(i * r - 1, 0), 0)),
                  pl.BlockSpec((HALO, d), vec),
                  pl.BlockSpec((1, d), vec),
                  pl.BlockSpec((1, d), vec),
                  pl.BlockSpec((1, d), vec),
                  pl.BlockSpec((d, d), vec),
                  pl.BlockSpec((1, d), vec),
                  pl.BlockSpec((1, ts, d), lambda b, i: (b, i, 0)),
                  pl.BlockSpec((1, 1, d), lambda b, i: (b, 0, 0))],
        out_specs=pl.BlockSpec((1, ts, d), lambda b, i: (b, i, 0)),
        out_shape=jax.ShapeDtypeStruct((bsz, seq, d), F32),
        scratch_shapes=[pltpu.VMEM((HALO + ts, d), F32)],
        compiler_params=_cparams("parallel", "arbitrary"),
        name="conv_block",
    )(u3, u3, wdw, b_dw.reshape(1, d), ln_g.reshape(1, d), ln_b.reshape(1, d),
      w_out_bf16, b_out.reshape(1, d), x3, g1)


ATT_T = 256


def _attn_kernel(q_ref, k_ref, v_ref, o_ref):
    t = ATT_T
    qi = pl.program_id(2)
    q2 = q_ref[0] * (1.0 / math.sqrt(HEAD_DIM))
    lane = lax.broadcasted_iota(jnp.int32, (1, LANES), 1)
    head_mask = (lane < HEAD_DIM, lane >= HEAD_DIM)
    row = lax.broadcasted_iota(jnp.int32, (t, t), 0)
    col = lax.broadcasted_iota(jnp.int32, (t, t), 1)
    tri = (row >= col).astype(BF16)
    tri2 = jnp.concatenate([tri, tri], axis=0)
    qh = tuple(jnp.where(m, q2, jnp.zeros_like(q2)) for m in head_mask)
    nt = (((1,), (1,)), ((), ()))

    def tile(kt, suf, acc, diag):
        start = pl.multiple_of(kt * t, t)
        k2 = k_ref[0, pl.ds(start, t), :]
        v2 = v_ref[0, pl.ds(start, t), :]
        new_suf = []
        for h in range(2):
            z = lax.dot_general(qh[h], k2, nt, preferred_element_type=F32)
            sp = jnp.maximum(z, 0.0) + jnp.log(1.0 + jnp.exp(-jnp.abs(z)))
            if diag:
                sp = jnp.where(col < row, sp, 0.0)
            hi = sp.astype(BF16)
            mid = (sp - hi.astype(F32)).astype(BF16)
            cs = jnp.dot(jnp.concatenate([hi, mid], axis=1), tri2,
                         preferred_element_type=F32)
            a = jnp.exp(z - cs - suf[h])
            if diag:
                a = jnp.where(col < row, a, 0.0)
            pv = jnp.dot(a.astype(BF16), v2, preferred_element_type=F32)
            acc = acc + jnp.where(head_mask[h], pv, 0.0)
            new_suf.append(suf[h] + cs[:, 0:1])
        return tuple(new_suf), acc

    zero = jnp.zeros((t, 1), F32)
    suf, acc = tile(qi, (zero, zero), jnp.zeros((t, LANES), F32), True)

    def body(n, carry):
        s0, s1, acc = carry
        (s0, s1), acc = tile(qi - n, (s0, s1), acc, False)
        return s0, s1, acc

    _, _, acc = lax.fori_loop(1, qi + 1, body, (suf[0], suf[1], acc))
    o_ref[0] = acc


def _attention(qkv3):
    bsz, seq, n3 = qkv3.shape
    d = n3 // 3
    nb = d // LANES
    return pl.pallas_call(
        _attn_kernel,
        grid=(bsz, nb, seq // ATT_T),
        in_specs=[pl.BlockSpec((1, ATT_T, LANES), lambda b, h, i: (b, i, h)),
                  pl.BlockSpec((1, seq, LANES), lambda b, h, i: (b, 0, nb + h)),
                  pl.BlockSpec((1, seq, LANES), lambda b, h, i: (b, 0, 2 * nb + h))],
        out_specs=pl.BlockSpec((1, ATT_T, LANES), lambda b, h, i: (b, i, h)),
        out_shape=jax.ShapeDtypeStruct((bsz, seq, d), F32),
        compiler_params=_cparams("parallel", "parallel", "arbitrary"),
        name="sb_attention",
    )(qkv3, qkv3, qkv3)


def _mm_res_kernel(a_ref, w_ref, x_ref, g_ref, o_ref):
    m = jnp.dot(a_ref[...].astype(BF16), w_ref[...], preferred_element_type=F32)
    o_ref[...] = x_ref[...] + g_ref[0] * m


def _mm_res(a2, w_bf16, x2, g, seq, tm=512):
    t, d = x2.shape
    k = a2.shape[1]
    return pl.pallas_call(
        _mm_res_kernel,
        grid=(t // tm,),
        in_specs=[pl.BlockSpec((tm, k), lambda i: (i, 0)),
                  pl.BlockSpec((k, d), lambda i: (0, 0)),
                  pl.BlockSpec((tm, d), lambda i: (i, 0)),
                  pl.BlockSpec((1, 1, d), lambda i: ((i * tm) // seq, 0, 0))],
        out_specs=pl.BlockSpec((tm, d), lambda i: (i, 0)),
        out_shape=jax.ShapeDtypeStruct((t, d), F32),
        compiler_params=_cparams("parallel"),
        name="mm_res",
    )(a2, w_bf16, x2, g)


PAIR_ROWS = tuple(PEER_TOPK // (i + 1) for i in range(PEER_TOPK))


def _extract_top(s, payload, k):
    rows = lax.broadcasted_iota(jnp.int32, s.shape, 0)
    big = s.shape[0]
    vals, outs = [], []
    for _ in range(k):
        m = jnp.max(s, axis=0, keepdims=True)
        r = jnp.min(jnp.where(s == m, rows, big), axis=0, keepdims=True)
        sel = rows == r
        vals.append(m)
        if payload is None:
            outs.append(r)
        else:
            outs.append(jnp.max(jnp.where(sel, payload, -1), axis=0, keepdims=True))
        s = jnp.where(sel, NEG_INF, s)
    return jnp.concatenate(vals, axis=0), jnp.concatenate(outs, axis=0)


def _pair_candidates(a, ia, b, ib):
    sub = lax.broadcasted_iota(jnp.int32, (SUBLANES, a.shape[1]), 0)
    cand, ids = [], []
    for i in range(SUBLANES):
        for j0 in range(0, PAIR_ROWS[i], SUBLANES):
            n = min(PAIR_ROWS[i] - j0, SUBLANES)
            c = a[i:i + 1, :] + b[j0:j0 + SUBLANES, :]
            e = ia[i:i + 1, :] * N_KEYS + ib[j0:j0 + SUBLANES, :]
            if n < SUBLANES:
                c = jnp.where(sub < n, c, NEG_INF)
            cand.append(c)
            ids.append(e)
    cand.append(a[SUBLANES:, :] + b[0:1, :])
    ids.append(ia[SUBLANES:, :] * N_KEYS + ib[0:1, :])
    return jnp.concatenate(cand, axis=0), jnp.concatenate(ids, axis=0)


def _peer_route_kernel(row_base, x_ref, gn_ref, sc_ref, sh_ref, wq_ref, keys_ref,
                       h_ref, e_ref, g_ref, q_s, e_s, g_s):
    h = _norm_mod(x_ref[...], gn_ref[...], sc_ref[0], sh_ref[0])
    h_ref[...] = h
    q = jnp.dot(h.astype(BF16), wq_ref[...], preferred_element_type=F32)
    for j in range(2 * PEER_HEADS):
        q_s[j] = q[:, j * PEER_DQH:(j + 1) * PEER_DQH].astype(BF16)
    nt = (((1,), (1,)), ((), ()))

    def head(hd, carry):
        tops = []
        for c in range(2):
            s_t = lax.dot_general(keys_ref[2 * hd + c], q_s[2 * hd + c], nt,
                                  preferred_element_type=F32)
            tops.append(_extract_top(s_t, None, PEER_TOPK))
        cand, ids = _pair_candidates(tops[0][0], tops[0][1], tops[1][0], tops[1][1])
        best, experts = _extract_top(cand, ids, PEER_TOPK)
        ex = jnp.exp(best - best[0:1, :])
        g_s[hd] = ex / jnp.sum(ex, axis=0, keepdims=True)
        e_s[hd] = experts
        return carry

    lax.fori_loop(0, PEER_HEADS, head, 0)
    tm = x_ref.shape[0]
    e_ref[...] = e_s[...].reshape(PEER_HK, tm).T + row_base
    g_ref[...] = g_s[...].reshape(PEER_HK, tm).T


def _peer_route(x2, seq, gn, sc, sh, wq_bf16, keys_bf16, row_base, tm=256):
    t, d = x2.shape
    nq = wq_bf16.shape[1]
    per_seq = lambda i: ((i * tm) // seq, 0, 0)
    return pl.pallas_call(
        functools.partial(_peer_route_kernel, row_base),
        grid=(t // tm,),
        in_specs=[pl.BlockSpec((tm, d), lambda i: (i, 0)),
                  pl.BlockSpec((1, d), lambda i: (0, 0)),
                  pl.BlockSpec((1, 1, d), per_seq),
                  pl.BlockSpec((1, 1, d), per_seq),
                  pl.BlockSpec((d, nq), lambda i: (0, 0)),
                  pl.BlockSpec((2 * PEER_HEADS, N_KEYS, PEER_DQH), lambda i: (0, 0, 0))],
        out_specs=[pl.BlockSpec((tm, d), lambda i: (i, 0)),
                   pl.BlockSpec((tm, PEER_HK), lambda i: (i, 0)),
                   pl.BlockSpec((tm, PEER_HK), lambda i: (i, 0))],
        out_shape=[jax.ShapeDtypeStruct((t, d), F32),
                   jax.ShapeDtypeStruct((t, PEER_HK), jnp.int32),
                   jax.ShapeDtypeStruct((t, PEER_HK), F32)],
        scratch_shapes=[pltpu.VMEM((2 * PEER_HEADS, tm, PEER_DQH), BF16),
                        pltpu.VMEM((PEER_HEADS, PEER_TOPK, tm), jnp.int32),
                        pltpu.VMEM((PEER_HEADS, PEER_TOPK, tm), F32)],
        compiler_params=_cparams("parallel"),
        name="peer_route",
    )(x2, gn.reshape(1, d), sc, sh, wq_bf16, keys_bf16)


SC_ROWS = 16
SC_GROUPS = PEER_HK // SC_ROWS
SC_TB = 16
SC_CHUNKS = D_MODEL // SC_LANES
SC_HALF = D_MODEL // 2
SC_WCH = SC_HALF // SC_LANES
SC_NBUF = 4


def _pack_table(tab):
    b = lax.bitcast_convert_type(tab.reshape(-1, D_MODEL).astype(BF16), jnp.uint16)
    b = b.astype(jnp.uint32)
    return lax.bitcast_convert_type(b[:, :SC_HALF] | (b[:, SC_HALF:] << 16), jnp.int32)


def _unpack_pair(w):
    lo = plsc.bitcast(lax.shift_left(w, jnp.int32(16)), F32)
    hi = plsc.bitcast(w & jnp.int32(-65536), F32)
    return lo, hi


def _sc_mesh():
    return plsc.VectorSubcoreMesh(core_axis_name="c", subcore_axis_name="s")


def _sc_pipeline(n_items, gather, compute):
    ahead = SC_NBUF - 1
    for i in range(ahead):
        gather(i, i).start()

    @pl.loop(0, n_items, step=SC_NBUF)
    def _(it):
        for b in range(SC_NBUF):
            i = it + b
            gather(i, b).wait()

            @pl.when(i + ahead < n_items)
            def _():
                gather(i + ahead, (b + ahead) % SC_NBUF).start()

            compute(i, b)


def _tree_sum(terms):
    while len(terms) > 1:
        terms = [terms[i] + terms[i + 1] for i in range(0, len(terms), 2)]
    return terms[0]


def _peer_pre(h2, idx, u_tab, t):
    d = h2.shape[1]
    info = plsc.get_sparse_core_info()
    nc, nw = info.num_cores, info.num_cores * info.num_subcores
    tpw = t // nw
    n_items = SC_TB * SC_GROUPS

    @functools.partial(
        pl.kernel, mesh=_sc_mesh(),
        out_type=jax.ShapeDtypeStruct((t, PEER_HK), F32),
        scratch_types=[pltpu.VMEM((SC_TB, PEER_HK), jnp.int32),
                       pltpu.VMEM((SC_TB, d), F32),
                       pltpu.VMEM((SC_TB, PEER_HK), F32),
                       pltpu.VMEM((SC_ROWS, SC_LANES), F32)]
                      + [pltpu.VMEM((SC_ROWS, SC_HALF), jnp.int32)] * SC_NBUF
                      + [pltpu.SemaphoreType.DMA] * SC_NBUF,
        compiler_params=pltpu.CompilerParams(needs_layout_passes=False),
        name="peer_pre")
    def k(h_hbm, idx_hbm, u_hbm, p_hbm, idx_v, h_v, p_v, tr_v, *ring):
        wid = lax.axis_index("s") * nc + lax.axis_index("c")
        bufs, sems = ring[:SC_NBUF], ring[SC_NBUF:]
        lane = lax.iota(jnp.int32, SC_LANES)

        def gather(it, b):
            tl, g = it // SC_GROUPS, it % SC_GROUPS
            return pltpu.make_async_copy(
                u_hbm.at[idx_v.at[tl, pl.ds(g * SC_ROWS, SC_ROWS)]], bufs[b], sems[b])

        def compute(it, b):
            tl, g = it // SC_GROUPS, it % SC_GROUPS
            buf = bufs[b]

            def body(c, accs):
                sl = pl.ds(c * SC_LANES, SC_LANES)
                x_lo = h_v[tl, sl]
                x_hi = h_v[tl, pl.ds(SC_HALF + c * SC_LANES, SC_LANES)]
                out = []
                for r in range(SC_ROWS):
                    lo, hi = _unpack_pair(buf[r, sl])
                    out.append(accs[r] + (lo * x_lo + hi * x_hi))
                return tuple(out)

            zero = jnp.zeros((SC_LANES,), F32)
            accs = lax.fori_loop(0, SC_WCH, body, (zero,) * SC_ROWS)
            skew = [(lane + j) & (SC_LANES - 1) for j in range(SC_LANES)]
            for r in range(SC_ROWS):
                plsc.store_scatter(tr_v, [jnp.full((SC_LANES,), r, jnp.int32), skew[r]], accs[r])
            cols = [plsc.load_gather(tr_v, [lane, skew[j]]) for j in range(SC_LANES)]
            p_v[tl, pl.ds(g * SC_ROWS, SC_ROWS)] = _tree_sum(cols)

        @pl.loop(0, tpw // SC_TB)
        def _(blk):
            t0 = wid * tpw + blk * SC_TB
            pltpu.sync_copy(idx_hbm.at[pl.ds(t0, SC_TB)], idx_v)
            pltpu.sync_copy(h_hbm.at[pl.ds(t0, SC_TB)], h_v)
            _sc_pipeline(n_items, gather, compute)
            pltpu.sync_copy(p_v, p_hbm.at[pl.ds(t0, SC_TB)])

    return k(h2, idx, u_tab)


def _peer_post(act, idx, v_tab):
    t = act.shape[0]
    d = 2 * v_tab.shape[1]
    info = plsc.get_sparse_core_info()
    nc, nw = info.num_cores, info.num_cores * info.num_subcores
    tpw = t // nw
    n_items = SC_TB * SC_GROUPS

    @functools.partial(
        pl.kernel, mesh=_sc_mesh(),
        out_type=jax.ShapeDtypeStruct((t, d), F32),
        scratch_types=[pltpu.VMEM((SC_TB, PEER_HK), jnp.int32),
                       pltpu.VMEM((SC_TB, PEER_HK), F32),
                       pltpu.VMEM((SC_TB, d), F32)]
                      + [pltpu.VMEM((SC_ROWS, SC_HALF), jnp.int32)] * SC_NBUF
                      + [pltpu.SemaphoreType.DMA] * SC_NBUF,
        compiler_params=pltpu.CompilerParams(needs_layout_passes=False),
        name="peer_post")
    def k(a_hbm, idx_hbm, v_hbm, y_hbm, idx_v, a_v, y_v, *ring):
        wid = lax.axis_index("s") * nc + lax.axis_index("c")
        bufs, sems = ring[:SC_NBUF], ring[SC_NBUF:]

        def gather(it, b):
            tl, g = it // SC_GROUPS, it % SC_GROUPS
            return pltpu.make_async_copy(
                v_hbm.at[idx_v.at[tl, pl.ds(g * SC_ROWS, SC_ROWS)]], bufs[b], sems[b])

        def compute(it, b):
            tl, g = it // SC_GROUPS, it % SC_GROUPS
            buf = bufs[b]
            tlv = jnp.full((SC_LANES,), tl, jnp.int32)
            w = tuple(plsc.load_gather(a_v, [tlv, jnp.full((SC_LANES,), g * SC_ROWS + r, jnp.int32)])
                      for r in range(SC_ROWS))

            @plsc.parallel_loop(0, SC_WCH, unroll=2)
            def _(c):
                sl = pl.ds(c * SC_LANES, SC_LANES)
                parts = [_unpack_pair(buf[r, sl]) for r in range(SC_ROWS)]
                plsc.addupdate(y_v.at[tl, sl],
                               _tree_sum([w[r] * parts[r][0] for r in range(SC_ROWS)]))
                plsc.addupdate(y_v.at[tl, pl.ds(SC_HALF + c * SC_LANES, SC_LANES)],
                               _tree_sum([w[r] * parts[r][1] for r in range(SC_ROWS)]))

        @pl.loop(0, tpw // SC_TB)
        def _(blk):
            t0 = wid * tpw + blk * SC_TB
            pltpu.sync_copy(idx_hbm.at[pl.ds(t0, SC_TB)], idx_v)
            pltpu.sync_copy(a_hbm.at[pl.ds(t0, SC_TB)], a_v)

            @pl.loop(0, SC_TB)
            def _(r):
                @plsc.parallel_loop(0, SC_CHUNKS, unroll=4)
                def _(c):
                    y_v[r, pl.ds(c * SC_LANES, SC_LANES)] = jnp.zeros((SC_LANES,), F32)

            _sc_pipeline(n_items, gather, compute)
            pltpu.sync_copy(y_v, y_hbm.at[pl.ds(t0, SC_TB)])

    return k(act, idx, v_tab)


def _act_kernel(p_ref, g_ref, o_ref):
    p = p_ref[...]
    o_ref[...] = (p * (lax.erf(p * (1.0 / math.sqrt(2.0))) + 1.0) * 0.5) * g_ref[...]


def _expert_act(p, gates, tm=512):
    t, n = p.shape
    spec = pl.BlockSpec((tm, n), lambda i: (i, 0))
    return pl.pallas_call(
        _act_kernel, grid=(t // tm,), in_specs=[spec, spec], out_specs=spec,
        out_shape=jax.ShapeDtypeStruct((t, n), F32),
        compiler_params=_cparams("parallel"), name="expert_act",
    )(p, gates)


def _res_kernel(x_ref, y_ref, g_ref, o_ref):
    o_ref[...] = x_ref[...] + g_ref[0] * y_ref[...]


def _res_norm_kernel(x_ref, y_ref, g_ref, fg_ref, o_ref):
    x = x_ref[...] + g_ref[0] * y_ref[...]
    ms = jnp.mean(x * x, axis=-1, keepdims=True)
    o_ref[...] = x * lax.rsqrt(ms + EPS) * fg_ref[...]


def _residual(x2, y2, g, seq, final_g=None, tm=512):
    t, d = x2.shape
    tile = pl.BlockSpec((tm, d), lambda i: (i, 0))
    gspec = pl.BlockSpec((1, 1, d), lambda i: ((i * tm) // seq, 0, 0))
    if final_g is None:
        kern, ins, args = _res_kernel, [tile, tile, gspec], (x2, y2, g)
    else:
        kern = _res_norm_kernel
        ins = [tile, tile, gspec, pl.BlockSpec((1, d), lambda i: (0, 0))]
        args = (x2, y2, g, final_g.reshape(1, d))
    return pl.pallas_call(
        kern, grid=(t // tm,), in_specs=ins, out_specs=tile,
        out_shape=jax.ShapeDtypeStruct((t, d), F32),
        compiler_params=_cparams("parallel"), name="residual",
    )(*args)


TC_TT = 32
TC_TOKENS = 512


def _unpack_pair_tc(w):
    lo = lax.bitcast_convert_type(lax.shift_left(w, jnp.int32(16)), F32)
    hi = lax.bitcast_convert_type(w & jnp.int32(-65536), F32)
    return lo, hi


def _peer_tc_kernel(e_ref, h_ref, gt_ref, uv_hbm, o_ref, buf, sem):
    def issue(j, slot):
        def body(k, carry):
            e = e_ref[j, k]
            pltpu.make_async_copy(uv_hbm.at[pl.ds(e, 1), :], buf.at[slot, pl.ds(k, 1), :],
                                  sem.at[slot]).start()
            return carry

        lax.fori_loop(0, PEER_HK, body, 0, unroll=8)

    def wait(slot):
        pltpu.make_async_copy(uv_hbm.at[pl.ds(0, PEER_HK), :], buf.at[slot], sem.at[slot]).wait()

    lane_t = lax.broadcasted_iota(jnp.int32, (PEER_HK, TC_TT), 1)
    gt = gt_ref[0]
    issue(0, 0)

    def token(j, carry):
        slot = j % 2

        @pl.when(j + 1 < TC_TT)
        def _():
            issue(j + 1, 1 - slot)

        wait(slot)
        x = h_ref[pl.ds(j, 1), :]
        lo, hi = _unpack_pair_tc(buf[slot, :, 0:SC_HALF])
        p = jnp.sum(lo * x[:, :SC_HALF] + hi * x[:, SC_HALF:], axis=1, keepdims=True)
        gate = jnp.sum(jnp.where(lane_t == j, gt, 0.0), axis=1, keepdims=True)
        act = (p * (lax.erf(p * (1.0 / math.sqrt(2.0))) + 1.0) * 0.5) * gate
        lo, hi = _unpack_pair_tc(buf[slot, :, SC_HALF:])
        o_ref[pl.ds(j, 1), 0:SC_HALF] = jnp.sum(act * lo, axis=0, keepdims=True)
        o_ref[pl.ds(j, 1), SC_HALF:] = jnp.sum(act * hi, axis=0, keepdims=True)
        return carry

    lax.fori_loop(0, TC_TT, token, 0)


def _peer_tc(h2, experts, gates, uv_tab):
    t, d = h2.shape
    n = t // TC_TT
    gates_t = gates.reshape(n, TC_TT, PEER_HK).transpose(0, 2, 1)
    return pl.pallas_call(
        _peer_tc_kernel,
        grid=(n,),
        in_specs=[pl.BlockSpec((TC_TT, PEER_HK), lambda i: (i, 0), memory_space=pltpu.SMEM),
                  pl.BlockSpec((TC_TT, d), lambda i: (i, 0)),
                  pl.BlockSpec((1, PEER_HK, TC_TT), lambda i: (i, 0, 0)),
                  pl.BlockSpec(memory_space=pl.ANY)],
        out_specs=pl.BlockSpec((TC_TT, d), lambda i: (i, 0)),
        out_shape=jax.ShapeDtypeStruct((t, d), F32),
        scratch_shapes=[pltpu.VMEM((2, PEER_HK, d), jnp.int32),
                        pltpu.SemaphoreType.DMA((2,))],
        compiler_params=_cparams("arbitrary"),
        name="peer_tc",
    )(experts, h2, gates_t, uv_tab)


def _peer(x2, seq, gn, sc, sh, wq_bf16, keys_bf16, u_all, v_all, uv_all, layer):
    rows = _peer_route(x2, seq, gn, sc, sh, wq_bf16, keys_bf16, layer * N_KEYS * N_KEYS)
    h2, experts, gates = rows
    t_sc = h2.shape[0] - TC_TOKENS
    p = _peer_pre(h2, experts, u_all, t_sc)
    act = _expert_act(p, gates)
    y_sc = _peer_post(act, experts, v_all)
    y_tc = _peer_tc(h2[t_sc:], experts[t_sc:], gates[t_sc:], uv_all)
    return jnp.concatenate([y_sc, y_tc], axis=0)


BATCH_SPLIT = 4


def kernel(x, c, ada_w, ada_b, norm_mix_g, norm_ffn_g, cv_w_in, cv_b_in, cv_w_dw, cv_b_dw, cv_ln_g, cv_ln_b, cv_w_out, cv_b_out, sb_w_qkv, sb_w_o, pk_w_q, pk_keys, pk_u, pk_v, final_g):
    bsz, seq, d = x.shape
    mod = _ada_mod(c, ada_w, ada_b)
    w_in, w_out = cv_w_in.astype(BF16), cv_w_out.astype(BF16)
    w_qkv, w_o = sb_w_qkv.astype(BF16), sb_w_o.astype(BF16)
    w_q = pk_w_q.astype(BF16)
    keys = pk_keys.reshape(DEPTH, 2 * PEER_HEADS, N_KEYS, PEER_DQH).astype(BF16)
    u_all, v_all = _pack_table(pk_u), _pack_table(pk_v)
    uv_all = jnp.concatenate([u_all, v_all], axis=1)
    nb = bsz // BATCH_SPLIT
    t = nb * seq
    outs = []
    for part in range(BATCH_SPLIT):
        lo = part * nb
        x2 = x[lo:lo + nb].reshape(t, d)
        for i in range(DEPTH):
            sh1, sc1, g1, sh2, sc2, g2 = (
                mod[i, lo:lo + nb, n * d:(n + 1) * d].reshape(nb, 1, d) for n in range(ADA_CHUNKS))
            j = i // 2
            if i % 2 == 0:
                u2 = _norm_glu(x2, seq, norm_mix_g[i], sc1, sh1, w_in[j], cv_b_in[j])
                x2 = _conv_block(u2.reshape(nb, seq, d), x2.reshape(nb, seq, d), g1,
                                 cv_w_dw[j], cv_b_dw[j], cv_ln_g[j], cv_ln_b[j],
                                 w_out[j], cv_b_out[j]).reshape(t, d)
            else:
                qkv = _norm_mm(x2, seq, norm_mix_g[i], sc1, sh1, w_qkv[j], BF16)
                o = _attention(qkv.reshape(nb, seq, 3 * d))
                x2 = _mm_res(o.reshape(t, d), w_o[j], x2, g1, seq)
            y2 = _peer(x2, seq, norm_ffn_g[i], sc2, sh2, w_q[i], keys[i],
                       u_all, v_all, uv_all, i)
            x2 = _residual(x2, y2, g2, seq, final_g if i == DEPTH - 1 else None)
        outs.append(x2.reshape(nb, seq, d))
    return jnp.concatenate(outs, axis=0)
```

```python
import functools
import math

import jax
import jax.numpy as jnp
from jax import lax
from jax.experimental import pallas as pl
from jax.experimental.pallas import tpu as pltpu
from jax.experimental.pallas import tpu_sc as plsc

D_MODEL = 1024
DEPTH = 2
CONV_WIDTH = 31
N_HEADS = 16
HEAD_DIM = 64
PEER_HEADS = 8
PEER_DQH = 128
N_KEYS = 128
PEER_TOPK = 16
PEER_HK = PEER_HEADS * PEER_TOPK
ADA_CHUNKS = 6
EPS = 1e-6

LANES = 128
SUBLANES = 8
SC_LANES = 16
VMEM_LIMIT = 48 * 1024 * 1024

F32 = jnp.float32
BF16 = jnp.bfloat16
NEG_INF = float("-inf")


def _cparams(*sem):
    return pltpu.CompilerParams(dimension_semantics=sem, vmem_limit_bytes=VMEM_LIMIT)


def _norm_mod(x, gn, sc, sh):
    ms = jnp.mean(x * x, axis=-1, keepdims=True)
    return (x * lax.rsqrt(ms + EPS)) * gn * (1.0 + sc) + sh


def _ada_kernel(c_ref, w_ref, b_ref, o_ref):
    c = c_ref[...]
    ca = c * jax.nn.sigmoid(c)
    o_ref[0] = jnp.dot(ca.astype(BF16), w_ref[0].astype(BF16),
                       preferred_element_type=F32) + b_ref[0]


def _ada_mod(c, ada_w, ada_b):
    depth, d, n = ada_w.shape
    bsz = c.shape[0]
    tn = 512
    return pl.pallas_call(
        _ada_kernel,
        grid=(depth, n // tn),
        in_specs=[pl.BlockSpec((bsz, d), lambda i, j: (0, 0)),
                  pl.BlockSpec((1, d, tn), lambda i, j: (i, 0, j)),
                  pl.BlockSpec((1, 1, tn), lambda i, j: (i, 0, j))],
        out_specs=pl.BlockSpec((1, bsz, tn), lambda i, j: (i, 0, j)),
        out_shape=jax.ShapeDtypeStruct((depth, bsz, n), F32),
        compiler_params=_cparams("parallel", "parallel"),
        name="ada_mod",
    )(c, ada_w, ada_b.reshape(depth, 1, n))


def _norm_mm_kernel(x_ref, gn_ref, sc_ref, sh_ref, w_ref, o_ref):
    h = _norm_mod(x_ref[...], gn_ref[...], sc_ref[0], sh_ref[0])
    o_ref[...] = jnp.dot(h.astype(BF16), w_ref[...],
                         preferred_element_type=F32).astype(o_ref.dtype)


def _norm_mm(x2, seq, gn, sc, sh, w_bf16, out_dtype, tm=256):
    t, d = x2.shape
    n = w_bf16.shape[1]
    per_seq = lambda i: ((i * tm) // seq, 0, 0)
    return pl.pallas_call(
        _norm_mm_kernel,
        grid=(t // tm,),
        in_specs=[pl.BlockSpec((tm, d), lambda i: (i, 0)),
                  pl.BlockSpec((1, d), lambda i: (0, 0)),
                  pl.BlockSpec((1, 1, d), per_seq),
                  pl.BlockSpec((1, 1, d), per_seq),
                  pl.BlockSpec((d, n), lambda i: (0, 0))],
        out_specs=pl.BlockSpec((tm, n), lambda i: (i, 0)),
        out_shape=jax.ShapeDtypeStruct((t, n), out_dtype),
        compiler_params=_cparams("parallel"),
        name="norm_mm",
    )(x2, gn.reshape(1, d), sc, sh, w_bf16)


def _norm_glu_kernel(x_ref, gn_ref, sc_ref, sh_ref, wa_ref, wg_ref, ba_ref, bg_ref, o_ref):
    h = _norm_mod(x_ref[...], gn_ref[...], sc_ref[0], sh_ref[0]).astype(BF16)
    a = jnp.dot(h, wa_ref[...], preferred_element_type=F32) + ba_ref[...]
    g = jnp.dot(h, wg_ref[...], preferred_element_type=F32) + bg_ref[...]
    o_ref[...] = a * jax.nn.sigmoid(g)


def _norm_glu(x2, seq, gn, sc, sh, w_in_bf16, b_in, tm=256):
    t, d = x2.shape
    n = w_in_bf16.shape[1] // 2
    per_seq = lambda i: ((i * tm) // seq, 0, 0)
    b2 = b_in.reshape(1, 2 * n)
    return pl.pallas_call(
        _norm_glu_kernel,
        grid=(t // tm,),
        in_specs=[pl.BlockSpec((tm, d), lambda i: (i, 0)),
                  pl.BlockSpec((1, d), lambda i: (0, 0)),
                  pl.BlockSpec((1, 1, d), per_seq),
                  pl.BlockSpec((1, 1, d), per_seq),
                  pl.BlockSpec((d, n), lambda i: (0, 0)),
                  pl.BlockSpec((d, n), lambda i: (0, 1)),
                  pl.BlockSpec((1, n), lambda i: (0, 0)),
                  pl.BlockSpec((1, n), lambda i: (0, 1))],
        out_specs=pl.BlockSpec((tm, n), lambda i: (i, 0)),
        out_shape=jax.ShapeDtypeStruct((t, n), F32),
        compiler_params=_cparams("parallel"),
        name="norm_glu",
    )(x2, gn.reshape(1, d), sc, sh, w_in_bf16, w_in_bf16, b2, b2)


HALO = 32


def _conv_kernel(cur_ref, prev_ref, wdw_ref, bdw_ref, lng_ref, lnb_ref, wout_ref,
                 bout_ref, x_ref, g_ref, o_ref, win_s):
    ts = cur_ref.shape[1]
    first = pl.program_id(1) == 0
    win_s[0:HALO, :] = jnp.where(first, 0.0, prev_ref[0])
    win_s[HALO:HALO + ts, :] = cur_ref[0]
    acc = jnp.zeros(cur_ref.shape[1:], F32) + bdw_ref[...]
    off = HALO - (CONV_WIDTH - 1)
    for k in range(CONV_WIDTH):
        acc = acc + win_s[off + k:off + k + ts, :] * wdw_ref[k:k + 1, :]
    mu = jnp.mean(acc, axis=-1, keepdims=True)
    cen = acc - mu
    var = jnp.mean(cen * cen, axis=-1, keepdims=True)
    y = cen * lax.rsqrt(var + EPS) * lng_ref[...] + lnb_ref[...]
    y = y * jax.nn.sigmoid(y)
    m = jnp.dot(y.astype(BF16), wout_ref[...], preferred_element_type=F32) + bout_ref[...]
    o_ref[0] = x_ref[0] + g_ref[0] * m


def _conv_block(u3, x3, g1, w_dw, b_dw, ln_g, ln_b, w_out_bf16, b_out, ts=256):
    bsz, seq, d = u3.shape
    r = ts // HALO
    wdw = jnp.concatenate([w_dw, jnp.zeros((HALO - CONV_WIDTH, d), F32)], axis=0)
    vec = lambda b, i: (0, 0)
    return pl.pallas_call(
        _conv_kernel,
        grid=(bsz, seq // ts),
        in_specs=[pl.BlockSpec((1, ts, d), lambda b, i: (b, i, 0)),
                  pl.BlockSpec((1, HALO, d), lambda b, i: (b, jnp.maximum(i * r - 1, 0), 0)),
                  pl.BlockSpec((HALO, d), vec),
                  pl.BlockSpec((1, d), vec),
                  pl.BlockSpec((1, d), vec),
                  pl.BlockSpec((1, d), vec),
                  pl.BlockSpec((d, d), vec),
                  pl.BlockSpec((1, d), vec),
                  pl.BlockSpec((1, ts, d), lambda b, i: (b, i, 0)),
                  pl.BlockSpec((1, 1, d), lambda b, i: (b, 0, 0))],
        out_specs=pl.BlockSpec((1, ts, d), lambda b, i: (b, i, 0)),
        out_shape=jax.ShapeDtypeStruct((bsz, seq, d), F32),
        scratch_shapes=[pltpu.VMEM((HALO + ts, d), F32)],
        compiler_params=_cparams("parallel", "arbitrary"),
        name="conv_block",
    )(u3, u3, wdw, b_dw.reshape(1, d), ln_g.reshape(1, d), ln_b.reshape(1, d),
      w_out_bf16, b_out.reshape(1, d), x3, g1)


ATT_T = 256


def _attn_kernel(q_ref, k_ref, v_ref, o_ref):
    t = ATT_T
    qi = pl.program_id(2)
    q2 = q_ref[0] * (1.0 / math.sqrt(HEAD_DIM))
    lane = lax.broadcasted_iota(jnp.int32, (1, LANES), 1)
    head_mask = (lane < HEAD_DIM, lane >= HEAD_DIM)
    row = lax.broadcasted_iota(jnp.int32, (t, t), 0)
    col = lax.broadcasted_iota(jnp.int32, (t, t), 1)
    tri = (row >= col).astype(BF16)
    tri2 = jnp.concatenate([tri, tri], axis=0)
    qh = tuple(jnp.where(m, q2, jnp.zeros_like(q2)) for m in head_mask)
    nt = (((1,), (1,)), ((), ()))

    def tile(kt, suf, acc, diag):
        start = pl.multiple_of(kt * t, t)
        k2 = k_ref[0, pl.ds(start, t), :]
        v2 = v_ref[0, pl.ds(start, t), :]
        new_suf = []
        for h in range(2):
            z = lax.dot_general(qh[h], k2, nt, preferred_element_type=F32)
            sp = jnp.maximum(z, 0.0) + jnp.log(1.0 + jnp.exp(-jnp.abs(z)))
            if diag:
                sp = jnp.where(col < row, sp, 0.0)
            hi = sp.astype(BF16)
            mid = (sp - hi.astype(F32)).astype(BF16)
            cs = jnp.dot(jnp.concatenate([hi, mid], axis=1), tri2,
                         preferred_element_type=F32)
            a = jnp.exp(z - cs - suf[h])
            if diag:
                a = jnp.where(col < row, a, 0.0)
            pv = jnp.dot(a.astype(BF16), v2, preferred_element_type=F32)
            acc = acc + jnp.where(head_mask[h], pv, 0.0)
            new_suf.append(suf[h] + cs[:, 0:1])
        return tuple(new_suf), acc

    zero = jnp.zeros((t, 1), F32)
    suf, acc = tile(qi, (zero, zero), jnp.zeros((t, LANES), F32), True)

    def body(n, carry):
        s0, s1, acc = carry
        (s0, s1), acc = tile(qi - n, (s0, s1), acc, False)
        return s0, s1, acc

    _, _, acc = lax.fori_loop(1, qi + 1, body, (suf[0], suf[1], acc))
    o_ref[0] = acc


def _attention(qkv3):
    bsz, seq, n3 = qkv3.shape
    d = n3 // 3
    nb = d // LANES
    return pl.pallas_call(
        _attn_kernel,
        grid=(bsz, nb, seq // ATT_T),
        in_specs=[pl.BlockSpec((1, ATT_T, LANES), lambda b, h, i: (b, i, h)),
                  pl.BlockSpec((1, seq, LANES), lambda b, h, i: (b, 0, nb + h)),
                  pl.BlockSpec((1, seq, LANES), lambda b, h, i: (b, 0, 2 * nb + h))],
        out_specs=pl.BlockSpec((1, ATT_T, LANES), lambda b, h, i: (b, i, h)),
        out_shape=jax.ShapeDtypeStruct((bsz, seq, d), F32),
        compiler_params=_cparams("parallel", "parallel", "arbitrary"),
        name="sb_attention",
    )(qkv3, qkv3, qkv3)


def _mm_res_kernel(a_ref, w_ref, x_ref, g_ref, o_ref):
    m = jnp.dot(a_ref[...].astype(BF16), w_ref[...], preferred_element_type=F32)
    o_ref[...] = x_ref[...] + g_ref[0] * m


def _mm_res(a2, w_bf16, x2, g, seq, tm=512):
    t, d = x2.shape
    k = a2.shape[1]
    return pl.pallas_call(
        _mm_res_kernel,
        grid=(t // tm,),
        in_specs=[pl.BlockSpec((tm, k), lambda i: (i, 0)),
                  pl.BlockSpec((k, d), lambda i: (0, 0)),
                  pl.BlockSpec((tm, d), lambda i: (i, 0)),
                  pl.BlockSpec((1, 1, d), lambda i: ((i * tm) // seq, 0, 0))],
        out_specs=pl.BlockSpec((tm, d), lambda i: (i, 0)),
        out_shape=jax.ShapeDtypeStruct((t, d), F32),
        compiler_params=_cparams("parallel"),
        name="mm_res",
    )(a2, w_bf16, x2, g)


PAIR_ROWS = tuple(PEER_TOPK // (i + 1) for i in range(PEER_TOPK))


def _extract_top(s, payload, k):
    rows = lax.broadcasted_iota(jnp.int32, s.shape, 0)
    big = s.shape[0]
    vals, outs = [], []
    for _ in range(k):
        m = jnp.max(s, axis=0, keepdims=True)
        r = jnp.min(jnp.where(s == m, rows, big), axis=0, keepdims=True)
        sel = rows == r
        vals.append(m)
        if payload is None:
            outs.append(r)
        else:
            outs.append(jnp.max(jnp.where(sel, payload, -1), axis=0, keepdims=True))
        s = jnp.where(sel, NEG_INF, s)
    return jnp.concatenate(vals, axis=0), jnp.concatenate(outs, axis=0)


def _pair_candidates(a, ia, b, ib):
    sub = lax.broadcasted_iota(jnp.int32, (SUBLANES, a.shape[1]), 0)
    cand, ids = [], []
    for i in range(SUBLANES):
        for j0 in range(0, PAIR_ROWS[i], SUBLANES):
            n = min(PAIR_ROWS[i] - j0, SUBLANES)
            c = a[i:i + 1, :] + b[j0:j0 + SUBLANES, :]
            e = ia[i:i + 1, :] * N_KEYS + ib[j0:j0 + SUBLANES, :]
            if n < SUBLANES:
                c = jnp.where(sub < n, c, NEG_INF)
            cand.append(c)
            ids.append(e)
    cand.append(a[SUBLANES:, :] + b[0:1, :])
    ids.append(ia[SUBLANES:, :] * N_KEYS + ib[0:1, :])
    return jnp.concatenate(cand, axis=0), jnp.concatenate(ids, axis=0)


def _peer_route_kernel(row_base, x_ref, gn_ref, sc_ref, sh_ref, wq_ref, keys_ref,
                       h_ref, e_ref, g_ref, q_s, e_s, g_s):
    h = _norm_mod(x_ref[...], gn_ref[...], sc_ref[0], sh_ref[0])
    h_ref[...] = h
    q = jnp.dot(h.astype(BF16), wq_ref[...], preferred_element_type=F32)
    for j in range(2 * PEER_HEADS):
        q_s[j] = q[:, j * PEER_DQH:(j + 1) * PEER_DQH].astype(BF16)
    nt = (((1,), (1,)), ((), ()))

    def head(hd, carry):
        tops = []
        for c in range(2):
            s_t = lax.dot_general(keys_ref[2 * hd + c], q_s[2 * hd + c], nt,
                                  preferred_element_type=F32)
            tops.append(_extract_top(s_t, None, PEER_TOPK))
        cand, ids = _pair_candidates(tops[0][0], tops[0][1], tops[1][0], tops[1][1])
        best, experts = _extract_top(cand, ids, PEER_TOPK)
        ex = jnp.exp(best - best[0:1, :])
        g_s[hd] = ex / jnp.sum(ex, axis=0, keepdims=True)
        e_s[hd] = experts
        return carry

    lax.fori_loop(0, PEER_HEADS, head, 0)
    tm = x_ref.shape[0]
    e_ref[...] = e_s[...].reshape(PEER_HK, tm).T + row_base
    g_ref[...] = g_s[...].reshape(PEER_HK, tm).T


def _peer_route(x2, seq, gn, sc, sh, wq_bf16, keys_bf16, row_base, tm=256):
    t, d = x2.shape
    nq = wq_bf16.shape[1]
    per_seq = lambda i: ((i * tm) // seq, 0, 0)
    return pl.pallas_call(
        functools.partial(_peer_route_kernel, row_base),
        grid=(t // tm,),
        in_specs=[pl.BlockSpec((tm, d), lambda i: (i, 0)),
                  pl.BlockSpec((1, d), lambda i: (0, 0)),
                  pl.BlockSpec((1, 1, d), per_seq),
                  pl.BlockSpec((1, 1, d), per_seq),
                  pl.BlockSpec((d, nq), lambda i: (0, 0)),
                  pl.BlockSpec((2 * PEER_HEADS, N_KEYS, PEER_DQH), lambda i: (0, 0, 0))],
        out_specs=[pl.BlockSpec((tm, d), lambda i: (i, 0)),
                   pl.BlockSpec((tm, PEER_HK), lambda i: (i, 0)),
                   pl.BlockSpec((tm, PEER_HK), lambda i: (i, 0))],
        out_shape=[jax.ShapeDtypeStruct((t, d), F32),
                   jax.ShapeDtypeStruct((t, PEER_HK), jnp.int32),
                   jax.ShapeDtypeStruct((t, PEER_HK), F32)],
        scratch_shapes=[pltpu.VMEM((2 * PEER_HEADS, tm, PEER_DQH), BF16),
                        pltpu.VMEM((PEER_HEADS, PEER_TOPK, tm), jnp.int32),
                        pltpu.VMEM((PEER_HEADS, PEER_TOPK, tm), F32)],
        compiler_params=_cparams("parallel"),
        name="peer_route",
    )(x2, gn.reshape(1, d), sc, sh, wq_bf16, keys_bf16)


SC_ROWS = 16
SC_GROUPS = PEER_HK // SC_ROWS
SC_TB = 32
SC_CHUNKS = D_MODEL // SC_LANES
SC_HALF = D_MODEL // 2
SC_WCH = SC_HALF // SC_LANES
SC_NBUF = 4


def _pack_table(tab):
    b = lax.bitcast_convert_type(tab.reshape(-1, D_MODEL).astype(BF16), jnp.uint16)
    b = b.astype(jnp.uint32)
    return lax.bitcast_convert_type(b[:, :SC_HALF] | (b[:, SC_HALF:] << 16), jnp.int32)


def _unpack_pair(w):
    lo = plsc.bitcast(lax.shift_left(w, jnp.int32(16)), F32)
    hi = plsc.bitcast(w & jnp.int32(-65536), F32)
    return lo, hi


def _pair_products(w, xbf):
    return _unpack_pair(plsc.bitcast(plsc.bitcast(w, BF16) * xbf, jnp.int32))


def _sc_mesh():
    return plsc.VectorSubcoreMesh(core_axis_name="c", subcore_axis_name="s")


def _sc_pipeline(n_items, gather, compute):
    ahead = SC_NBUF - 1
    for i in range(ahead):
        gather(i, i).start()

    @pl.loop(0, n_items, step=SC_NBUF)
    def _(it):
        for b in range(SC_NBUF):
            i = it + b
            gather(i, b).wait()

            @pl.when(i + ahead < n_items)
            def _():
                gather(i + ahead, (b + ahead) % SC_NBUF).start()

            compute(i, b)


def _tree_sum(terms):
    while len(terms) > 1:
        terms = [terms[i] + terms[i + 1] for i in range(0, len(terms), 2)]
    return terms[0]


def _peer_pre(h_pk, idx, u_tab):
    t = h_pk.shape[0]
    info = plsc.get_sparse_core_info()
    nc, nw = info.num_cores, info.num_cores * info.num_subcores
    tpw = t // nw
    n_items = SC_TB * SC_GROUPS

    @functools.partial(
        pl.kernel, mesh=_sc_mesh(),
        out_type=jax.ShapeDtypeStruct((t, PEER_HK), F32),
        scratch_types=[pltpu.VMEM((SC_TB, PEER_HK), jnp.int32),
                       pltpu.VMEM((SC_TB, SC_HALF), jnp.int32),
                       pltpu.VMEM((SC_TB, PEER_HK), F32),
                       pltpu.VMEM((SC_ROWS, SC_LANES), F32)]
                      + [pltpu.VMEM((SC_ROWS, SC_HALF), jnp.int32)] * SC_NBUF
                      + [pltpu.SemaphoreType.DMA] * SC_NBUF,
        compiler_params=pltpu.CompilerParams(needs_layout_passes=False),
        name="peer_pre")
    def k(h_hbm, idx_hbm, u_hbm, p_hbm, idx_v, h_v, p_v, tr_v, *ring):
        wid = lax.axis_index("s") * nc + lax.axis_index("c")
        bufs, sems = ring[:SC_NBUF], ring[SC_NBUF:]
        lane = lax.iota(jnp.int32, SC_LANES)

        def gather(it, b):
            tl, g = it // SC_GROUPS, it % SC_GROUPS
            return pltpu.make_async_copy(
                u_hbm.at[idx_v.at[tl, pl.ds(g * SC_ROWS, SC_ROWS)]], bufs[b], sems[b])

        def compute(it, b):
            tl, g = it // SC_GROUPS, it % SC_GROUPS
            buf = bufs[b]

            def body(c, accs):
                sl = pl.ds(c * SC_LANES, SC_LANES)
                xbf = plsc.bitcast(h_v[tl, sl], BF16)
                out = []
                for r in range(SC_ROWS):
                    lo, hi = _pair_products(buf[r, sl], xbf)
                    out.append(accs[r] + (lo + hi))
                return tuple(out)

            zero = jnp.zeros((SC_LANES,), F32)
            accs = lax.fori_loop(0, SC_WCH, body, (zero,) * SC_ROWS)
            skew = [(lane + j) & (SC_LANES - 1) for j in range(SC_LANES)]
            for r in range(SC_ROWS):
                plsc.store_scatter(tr_v, [jnp.full((SC_LANES,), r, jnp.int32), skew[r]], accs[r])
            cols = [plsc.load_gather(tr_v, [lane, skew[j]]) for j in range(SC_LANES)]
            p_v[tl, pl.ds(g * SC_ROWS, SC_ROWS)] = _tree_sum(cols)

        @pl.loop(0, tpw // SC_TB)
        def _(blk):
            t0 = wid * tpw + blk * SC_TB
            pltpu.sync_copy(idx_hbm.at[pl.ds(t0, SC_TB)], idx_v)
            pltpu.sync_copy(h_hbm.at[pl.ds(t0, SC_TB)], h_v)
            _sc_pipeline(n_items, gather, compute)
            pltpu.sync_copy(p_v, p_hbm.at[pl.ds(t0, SC_TB)])

    return k(h_pk, idx, u_tab)


def _peer_post(act, idx, v_tab):
    t = act.shape[0]
    d = 2 * v_tab.shape[1]
    info = plsc.get_sparse_core_info()
    nc, nw = info.num_cores, info.num_cores * info.num_subcores
    tpw = t // nw
    n_items = SC_TB * SC_GROUPS

    @functools.partial(
        pl.kernel, mesh=_sc_mesh(),
        out_type=jax.ShapeDtypeStruct((t, d), F32),
        scratch_types=[pltpu.VMEM((SC_TB, PEER_HK), jnp.int32),
                       pltpu.VMEM((SC_TB, PEER_HK), jnp.int32),
                       pltpu.VMEM((SC_TB, d), F32)]
                      + [pltpu.VMEM((SC_ROWS, SC_HALF), jnp.int32)] * SC_NBUF
                      + [pltpu.SemaphoreType.DMA] * SC_NBUF,
        compiler_params=pltpu.CompilerParams(needs_layout_passes=False),
        name="peer_post")
    def k(a_hbm, idx_hbm, v_hbm, y_hbm, idx_v, a_v, y_v, *ring):
        wid = lax.axis_index("s") * nc + lax.axis_index("c")
        bufs, sems = ring[:SC_NBUF], ring[SC_NBUF:]

        def gather(it, b):
            tl, g = it // SC_GROUPS, it % SC_GROUPS
            return pltpu.make_async_copy(
                v_hbm.at[idx_v.at[tl, pl.ds(g * SC_ROWS, SC_ROWS)]], bufs[b], sems[b])

        def compute(it, b):
            tl, g = it // SC_GROUPS, it % SC_GROUPS
            buf = bufs[b]
            tlv = jnp.full((SC_LANES,), tl, jnp.int32)
            w = tuple(plsc.bitcast(plsc.load_gather(
                a_v, [tlv, jnp.full((SC_LANES,), g * SC_ROWS + r, jnp.int32)]), BF16)
                for r in range(SC_ROWS))

            @plsc.parallel_loop(0, SC_WCH, unroll=2)
            def _(c):
                sl = pl.ds(c * SC_LANES, SC_LANES)
                parts = [_pair_products(buf[r, sl], w[r]) for r in range(SC_ROWS)]
                plsc.addupdate(y_v.at[tl, sl], _tree_sum([p[0] for p in parts]))
                plsc.addupdate(y_v.at[tl, pl.ds(SC_HALF + c * SC_LANES, SC_LANES)],
                               _tree_sum([p[1] for p in parts]))

        @pl.loop(0, tpw // SC_TB)
        def _(blk):
            t0 = wid * tpw + blk * SC_TB
            pltpu.sync_copy(idx_hbm.at[pl.ds(t0, SC_TB)], idx_v)
            pltpu.sync_copy(a_hbm.at[pl.ds(t0, SC_TB)], a_v)

            @pl.loop(0, SC_TB)
            def _(r):
                @plsc.parallel_loop(0, SC_CHUNKS, unroll=4)
                def _(c):
                    y_v[r, pl.ds(c * SC_LANES, SC_LANES)] = jnp.zeros((SC_LANES,), F32)

            _sc_pipeline(n_items, gather, compute)
            pltpu.sync_copy(y_v, y_hbm.at[pl.ds(t0, SC_TB)])

    return k(act, idx, v_tab)


def _act_kernel(p_ref, g_ref, o_ref):
    p = p_ref[...]
    o_ref[...] = (p * (lax.erf(p * (1.0 / math.sqrt(2.0))) + 1.0) * 0.5) * g_ref[...]


def _expert_act(p, gates, tm=512):
    t, n = p.shape
    spec = pl.BlockSpec((tm, n), lambda i: (i, 0))
    return pl.pallas_call(
        _act_kernel, grid=(t // tm,), in_specs=[spec, spec], out_specs=spec,
        out_shape=jax.ShapeDtypeStruct((t, n), F32),
        compiler_params=_cparams("parallel"), name="expert_act",
    )(p, gates)


def _res_kernel(x_ref, y_ref, g_ref, o_ref):
    o_ref[...] = x_ref[...] + g_ref[0] * y_ref[...]


def _res_norm_kernel(x_ref, y_ref, g_ref, fg_ref, o_ref):
    x = x_ref[...] + g_ref[0] * y_ref[...]
    ms = jnp.mean(x * x, axis=-1, keepdims=True)
    o_ref[...] = x * lax.rsqrt(ms + EPS) * fg_ref[...]


def _residual(x2, y2, g, seq, final_g=None, tm=512):
    t, d = x2.shape
    tile = pl.BlockSpec((tm, d), lambda i: (i, 0))
    gspec = pl.BlockSpec((1, 1, d), lambda i: ((i * tm) // seq, 0, 0))
    if final_g is None:
        kern, ins, args = _res_kernel, [tile, tile, gspec], (x2, y2, g)
    else:
        kern = _res_norm_kernel
        ins = [tile, tile, gspec, pl.BlockSpec((1, d), lambda i: (0, 0))]
        args = (x2, y2, g, final_g.reshape(1, d))
    return pl.pallas_call(
        kern, grid=(t // tm,), in_specs=ins, out_specs=tile,
        out_shape=jax.ShapeDtypeStruct((t, d), F32),
        compiler_params=_cparams("parallel"), name="residual",
    )(*args)


def _pack_same(a):
    b = lax.bitcast_convert_type(a.astype(BF16), jnp.uint16).astype(jnp.uint32)
    return lax.bitcast_convert_type(b | (b << 16), jnp.int32)


def _peer(x2, seq, gn, sc, sh, wq_bf16, keys_bf16, u_all, v_all, layer):
    rows = _peer_route(x2, seq, gn, sc, sh, wq_bf16, keys_bf16, layer * N_KEYS * N_KEYS)
    h2, experts, gates = rows
    p = _peer_pre(_pack_table(h2), experts, u_all)
    act = _expert_act(p, gates)
    return _peer_post(_pack_same(act), experts, v_all)


BATCH_SPLIT = 4


def kernel(x, c, ada_w, ada_b, norm_mix_g, norm_ffn_g, cv_w_in, cv_b_in, cv_w_dw, cv_b_dw, cv_ln_g, cv_ln_b, cv_w_out, cv_b_out, sb_w_qkv, sb_w_o, pk_w_q, pk_keys, pk_u, pk_v, final_g):
    bsz, seq, d = x.shape
    mod = _ada_mod(c, ada_w, ada_b)
    w_in, w_out = cv_w_in.astype(BF16), cv_w_out.astype(BF16)
    w_qkv, w_o = sb_w_qkv.astype(BF16), sb_w_o.astype(BF16)
    w_q = pk_w_q.astype(BF16)
    keys = pk_keys.reshape(DEPTH, 2 * PEER_HEADS, N_KEYS, PEER_DQH).astype(BF16)
    u_all, v_all = _pack_table(pk_u), _pack_table(pk_v)
    nb = bsz // BATCH_SPLIT
    t = nb * seq
    outs = []
    for part in range(BATCH_SPLIT):
        lo = part * nb
        x2 = x[lo:lo + nb].reshape(t, d)
        for i in range(DEPTH):
            sh1, sc1, g1, sh2, sc2, g2 = (
                mod[i, lo:lo + nb, n * d:(n + 1) * d].reshape(nb, 1, d) for n in range(ADA_CHUNKS))
            j = i // 2
            if i % 2 == 0:
                u2 = _norm_glu(x2, seq, norm_mix_g[i], sc1, sh1, w_in[j], cv_b_in[j])
                x2 = _conv_block(u2.reshape(nb, seq, d), x2.reshape(nb, seq, d), g1,
                                 cv_w_dw[j], cv_b_dw[j], cv_ln_g[j], cv_ln_b[j],
                                 w_out[j], cv_b_out[j]).reshape(t, d)
            else:
                qkv = _norm_mm(x2, seq, norm_mix_g[i], sc1, sh1, w_qkv[j], BF16)
                o = _attention(qkv.reshape(nb, seq, 3 * d))
                x2 = _mm_res(o.reshape(t, d), w_o[j], x2, g1, seq)
            y2 = _peer(x2, seq, norm_ffn_g[i], sc2, sh2, w_q[i], keys[i], u_all, v_all, i)
            x2 = _residual(x2, y2, g2, seq, final_g if i == DEPTH - 1 else None)
        outs.append(x2.reshape(nb, seq, d))
    return jnp.concatenate(outs, axis=0)
```

```python
import functools
import math

import jax
import jax.numpy as jnp
from jax import lax
from jax.experimental import pallas as pl
from jax.experimental.pallas import tpu as pltpu
from jax.experimental.pallas import tpu_sc as plsc

D_MODEL = 1024
DEPTH = 2
CONV_WIDTH = 31
N_HEADS = 16
HEAD_DIM = 64
PEER_HEADS = 8
PEER_DQH = 128
N_KEYS = 128
PEER_TOPK = 16
PEER_HK = PEER_HEADS * PEER_TOPK
ADA_CHUNKS = 6
EPS = 1e-6

LANES = 128
SUBLANES = 8
SC_LANES = 16
VMEM_LIMIT = 48 * 1024 * 1024

F32 = jnp.float32
BF16 = jnp.bfloat16
NEG_INF = float("-inf")


def _cparams(*sem):
    return pltpu.CompilerParams(dimension_semantics=sem, vmem_limit_bytes=VMEM_LIMIT)


def _norm_mod(x, gn, sc, sh):
    ms = jnp.mean(x * x, axis=-1, keepdims=True)
    return (x * lax.rsqrt(ms + EPS)) * gn * (1.0 + sc) + sh


def _ada_kernel(c_ref, w_ref, b_ref, o_ref):
    c = c_ref[...]
    ca = c * jax.nn.sigmoid(c)
    o_ref[0] = jnp.dot(ca.astype(BF16), w_ref[0].astype(BF16),
                       preferred_element_type=F32) + b_ref[0]


def _ada_mod(c, ada_w, ada_b):
    depth, d, n = ada_w.shape
    bsz = c.shape[0]
    tn = 512
    return pl.pallas_call(
        _ada_kernel,
        grid=(depth, n // tn),
        in_specs=[pl.BlockSpec((bsz, d), lambda i, j: (0, 0)),
                  pl.BlockSpec((1, d, tn), lambda i, j: (i, 0, j)),
                  pl.BlockSpec((1, 1, tn), lambda i, j: (i, 0, j))],
        out_specs=pl.BlockSpec((1, bsz, tn), lambda i, j: (i, 0, j)),
        out_shape=jax.ShapeDtypeStruct((depth, bsz, n), F32),
        compiler_params=_cparams("parallel", "parallel"),
        name="ada_mod",
    )(c, ada_w, ada_b.reshape(depth, 1, n))


def _norm_mm_kernel(x_ref, gn_ref, sc_ref, sh_ref, w_ref, o_ref):
    h = _norm_mod(x_ref[...], gn_ref[...], sc_ref[0], sh_ref[0])
    o_ref[...] = jnp.dot(h.astype(BF16), w_ref[...],
                         preferred_element_type=F32).astype(o_ref.dtype)


def _norm_mm(x2, seq, gn, sc, sh, w_bf16, out_dtype, tm=256):
    t, d = x2.shape
    n = w_bf16.shape[1]
    per_seq = lambda i: ((i * tm) // seq, 0, 0)
    return pl.pallas_call(
        _norm_mm_kernel,
        grid=(t // tm,),
        in_specs=[pl.BlockSpec((tm, d), lambda i: (i, 0)),
                  pl.BlockSpec((1, d), lambda i: (0, 0)),
                  pl.BlockSpec((1, 1, d), per_seq),
                  pl.BlockSpec((1, 1, d), per_seq),
                  pl.BlockSpec((d, n), lambda i: (0, 0))],
        out_specs=pl.BlockSpec((tm, n), lambda i: (i, 0)),
        out_shape=jax.ShapeDtypeStruct((t, n), out_dtype),
        compiler_params=_cparams("parallel"),
        name="norm_mm",
    )(x2, gn.reshape(1, d), sc, sh, w_bf16)


def _norm_glu_kernel(x_ref, gn_ref, sc_ref, sh_ref, wa_ref, wg_ref, ba_ref, bg_ref, o_ref):
    h = _norm_mod(x_ref[...], gn_ref[...], sc_ref[0], sh_ref[0]).astype(BF16)
    a = jnp.dot(h, wa_ref[...], preferred_element_type=F32) + ba_ref[...]
    g = jnp.dot(h, wg_ref[...], preferred_element_type=F32) + bg_ref[...]
    o_ref[...] = a * jax.nn.sigmoid(g)


def _norm_glu(x2, seq, gn, sc, sh, w_in_bf16, b_in, tm=256):
    t, d = x2.shape
    n = w_in_bf16.shape[1] // 2
    per_seq = lambda i: ((i * tm) // seq, 0, 0)
    b2 = b_in.reshape(1, 2 * n)
    return pl.pallas_call(
        _norm_glu_kernel,
        grid=(t // tm,),
        in_specs=[pl.BlockSpec((tm, d), lambda i: (i, 0)),
                  pl.BlockSpec((1, d), lambda i: (0, 0)),
                  pl.BlockSpec((1, 1, d), per_seq),
                  pl.BlockSpec((1, 1, d), per_seq),
                  pl.BlockSpec((d, n), lambda i: (0, 0)),
                  pl.BlockSpec((d, n), lambda i: (0, 1)),
                  pl.BlockSpec((1, n), lambda i: (0, 0)),
                  pl.BlockSpec((1, n), lambda i: (0, 1))],
        out_specs=pl.BlockSpec((tm, n), lambda i: (i, 0)),
        out_shape=jax.ShapeDtypeStruct((t, n), F32),
        compiler_params=_cparams("parallel"),
        name="norm_glu",
    )(x2, gn.reshape(1, d), sc, sh, w_in_bf16, w_in_bf16, b2, b2)


HALO = 32


def _conv_kernel(cur_ref, prev_ref, wdw_ref, bdw_ref, lng_ref, lnb_ref, wout_ref,
                 bout_ref, x_ref, g_ref, o_ref, win_s):
    ts = cur_ref.shape[1]
    first = pl.program_id(1) == 0
    win_s[0:HALO, :] = jnp.where(first, 0.0, prev_ref[0])
    win_s[HALO:HALO + ts, :] = cur_ref[0]
    acc = jnp.zeros(cur_ref.shape[1:], F32) + bdw_ref[...]
    off = HALO - (CONV_WIDTH - 1)
    for k in range(CONV_WIDTH):
        acc = acc + win_s[off + k:off + k + ts, :] * wdw_ref[k:k + 1, :]
    mu = jnp.mean(acc, axis=-1, keepdims=True)
    cen = acc - mu
    var = jnp.mean(cen * cen, axis=-1, keepdims=True)
    y = cen * lax.rsqrt(var + EPS) * lng_ref[...] + lnb_ref[...]
    y = y * jax.nn.sigmoid(y)
    m = jnp.dot(y.astype(BF16), wout_ref[...], preferred_element_type=F32) + bout_ref[...]
    o_ref[0] = x_ref[0] + g_ref[0] * m


def _conv_block(u3, x3, g1, w_dw, b_dw, ln_g, ln_b, w_out_bf16, b_out, ts=256):
    bsz, seq, d = u3.shape
    r = ts // HALO
    wdw = jnp.concatenate([w_dw, jnp.zeros((HALO - CONV_WIDTH, d), F32)], axis=0)
    vec = lambda b, i: (0, 0)
    return pl.pallas_call(
        _conv_kernel,
        grid=(bsz, seq // ts),
        in_specs=[pl.BlockSpec((1, ts, d), lambda b, i: (b, i, 0)),
                  pl.BlockSpec((1, HALO, d), lambda b, i: (b, jnp.maximum(i * r - 1, 0), 0)),
                  pl.BlockSpec((HALO, d), vec),
                  pl.BlockSpec((1, d), vec),
                  pl.BlockSpec((1, d), vec),
                  pl.BlockSpec((1, d), vec),
                  pl.BlockSpec((d, d), vec),
                  pl.BlockSpec((1, d), vec),
                  pl.BlockSpec((1, ts, d), lambda b, i: (b, i, 0)),
                  pl.BlockSpec((1, 1, d), lambda b, i: (b, 0, 0))],
        out_specs=pl.BlockSpec((1, ts, d), lambda b, i: (b, i, 0)),
        out_shape=jax.ShapeDtypeStruct((bsz, seq, d), F32),
        scratch_shapes=[pltpu.VMEM((HALO + ts, d), F32)],
        compiler_params=_cparams("parallel", "arbitrary"),
        name="conv_block",
    )(u3, u3, wdw, b_dw.reshape(1, d), ln_g.reshape(1, d), ln_b.reshape(1, d),
      w_out_bf16, b_out.reshape(1, d), x3, g1)


ATT_T = 256


def _attn_kernel(q_ref, k_ref, v_ref, o_ref):
    t = ATT_T
    qi = pl.program_id(2)
    q2 = q_ref[0] * (1.0 / math.sqrt(HEAD_DIM))
    lane = lax.broadcasted_iota(jnp.int32, (1, LANES), 1)
    head_mask = (lane < HEAD_DIM, lane >= HEAD_DIM)
    row = lax.broadcasted_iota(jnp.int32, (t, t), 0)
    col = lax.broadcasted_iota(jnp.int32, (t, t), 1)
    tri = (row >= col).astype(BF16)
    tri2 = jnp.concatenate([tri, tri], axis=0)
    qh = tuple(jnp.where(m, q2, jnp.zeros_like(q2)) for m in head_mask)
    nt = (((1,), (1,)), ((), ()))

    def tile(kt, suf, acc, diag):
        start = pl.multiple_of(kt * t, t)
        k2 = k_ref[0, pl.ds(start, t), :]
        v2 = v_ref[0, pl.ds(start, t), :]
        new_suf = []
        for h in range(2):
            z = lax.dot_general(qh[h], k2, nt, preferred_element_type=F32)
            sp = jnp.maximum(z, 0.0) + jnp.log(1.0 + jnp.exp(-jnp.abs(z)))
            if diag:
                sp = jnp.where(col < row, sp, 0.0)
            hi = sp.astype(BF16)
            mid = (sp - hi.astype(F32)).astype(BF16)
            cs = jnp.dot(jnp.concatenate([hi, mid], axis=1), tri2,
                         preferred_element_type=F32)
            a = jnp.exp(z - cs - suf[h])
            if diag:
                a = jnp.where(col < row, a, 0.0)
            pv = jnp.dot(a.astype(BF16), v2, preferred_element_type=F32)
            acc = acc + jnp.where(head_mask[h], pv, 0.0)
            new_suf.append(suf[h] + cs[:, 0:1])
        return tuple(new_suf), acc

    zero = jnp.zeros((t, 1), F32)
    suf, acc = tile(qi, (zero, zero), jnp.zeros((t, LANES), F32), True)

    def body(n, carry):
        s0, s1, acc = carry
        (s0, s1), acc = tile(qi - n, (s0, s1), acc, False)
        return s0, s1, acc

    _, _, acc = lax.fori_loop(1, qi + 1, body, (suf[0], suf[1], acc))
    o_ref[0] = acc


def _attention(qkv3):
    bsz, seq, n3 = qkv3.shape
    d = n3 // 3
    nb = d // LANES
    return pl.pallas_call(
        _attn_kernel,
        grid=(bsz, nb, seq // ATT_T),
        in_specs=[pl.BlockSpec((1, ATT_T, LANES), lambda b, h, i: (b, i, h)),
                  pl.BlockSpec((1, seq, LANES), lambda b, h, i: (b, 0, nb + h)),
                  pl.BlockSpec((1, seq, LANES), lambda b, h, i: (b, 0, 2 * nb + h))],
        out_specs=pl.BlockSpec((1, ATT_T, LANES), lambda b, h, i: (b, i, h)),
        out_shape=jax.ShapeDtypeStruct((bsz, seq, d), F32),
        compiler_params=_cparams("parallel", "parallel", "arbitrary"),
        name="sb_attention",
    )(qkv3, qkv3, qkv3)


def _mm_res_kernel(a_ref, w_ref, x_ref, g_ref, o_ref):
    m = jnp.dot(a_ref[...].astype(BF16), w_ref[...], preferred_element_type=F32)
    o_ref[...] = x_ref[...] + g_ref[0] * m


def _mm_res(a2, w_bf16, x2, g, seq, tm=512):
    t, d = x2.shape
    k = a2.shape[1]
    return pl.pallas_call(
        _mm_res_kernel,
        grid=(t // tm,),
        in_specs=[pl.BlockSpec((tm, k), lambda i: (i, 0)),
                  pl.BlockSpec((k, d), lambda i: (0, 0)),
                  pl.BlockSpec((tm, d), lambda i: (i, 0)),
                  pl.BlockSpec((1, 1, d), lambda i: ((i * tm) // seq, 0, 0))],
        out_specs=pl.BlockSpec((tm, d), lambda i: (i, 0)),
        out_shape=jax.ShapeDtypeStruct((t, d), F32),
        compiler_params=_cparams("parallel"),
        name="mm_res",
    )(a2, w_bf16, x2, g)


PAIR_ROWS = tuple(PEER_TOPK // (i + 1) for i in range(PEER_TOPK))


def _extract_top(s, payload, k):
    rows = lax.broadcasted_iota(jnp.int32, s.shape, 0)
    big = s.shape[0]
    vals, outs = [], []
    for _ in range(k):
        m = jnp.max(s, axis=0, keepdims=True)
        r = jnp.min(jnp.where(s == m, rows, big), axis=0, keepdims=True)
        sel = rows == r
        vals.append(m)
        if payload is None:
            outs.append(r)
        else:
            outs.append(jnp.max(jnp.where(sel, payload, -1), axis=0, keepdims=True))
        s = jnp.where(sel, NEG_INF, s)
    return jnp.concatenate(vals, axis=0), jnp.concatenate(outs, axis=0)


def _pair_candidates(a, ia, b, ib):
    sub = lax.broadcasted_iota(jnp.int32, (SUBLANES, a.shape[1]), 0)
    cand, ids = [], []
    for i in range(SUBLANES):
        for j0 in range(0, PAIR_ROWS[i], SUBLANES):
            n = min(PAIR_ROWS[i] - j0, SUBLANES)
            c = a[i:i + 1, :] + b[j0:j0 + SUBLANES, :]
            e = ia[i:i + 1, :] * N_KEYS + ib[j0:j0 + SUBLANES, :]
            if n < SUBLANES:
                c = jnp.where(sub < n, c, NEG_INF)
            cand.append(c)
            ids.append(e)
    cand.append(a[SUBLANES:, :] + b[0:1, :])
    ids.append(ia[SUBLANES:, :] * N_KEYS + ib[0:1, :])
    return jnp.concatenate(cand, axis=0), jnp.concatenate(ids, axis=0)


def _peer_route_kernel(row_base, x_ref, gn_ref, sc_ref, sh_ref, wq_ref, keys_ref,
                       h_ref, e_ref, g_ref, q_s, e_s, g_s):
    h = _norm_mod(x_ref[...], gn_ref[...], sc_ref[0], sh_ref[0])
    h_ref[...] = h
    q = jnp.dot(h.astype(BF16), wq_ref[...], preferred_element_type=F32)
    for j in range(2 * PEER_HEADS):
        q_s[j] = q[:, j * PEER_DQH:(j + 1) * PEER_DQH].astype(BF16)
    nt = (((1,), (1,)), ((), ()))

    def head(hd, carry):
        tops = []
        for c in range(2):
            s_t = lax.dot_general(keys_ref[2 * hd + c], q_s[2 * hd + c], nt,
                                  preferred_element_type=F32)
            tops.append(_extract_top(s_t, None, PEER_TOPK))
        cand, ids = _pair_candidates(tops[0][0], tops[0][1], tops[1][0], tops[1][1])
        best, experts = _extract_top(cand, ids, PEER_TOPK)
        ex = jnp.exp(best - best[0:1, :])
        g_s[hd] = ex / jnp.sum(ex, axis=0, keepdims=True)
        e_s[hd] = experts
        return carry

    lax.fori_loop(0, PEER_HEADS, head, 0)
    tm = x_ref.shape[0]
    e_ref[...] = e_s[...].reshape(PEER_HK, tm).T + row_base
    g_ref[...] = g_s[...].reshape(PEER_HK, tm).T


def _peer_route(x2, seq, gn, sc, sh, wq_bf16, keys_bf16, row_base, tm=256):
    t, d = x2.shape
    nq = wq_bf16.shape[1]
    per_seq = lambda i: ((i * tm) // seq, 0, 0)
    return pl.pallas_call(
        functools.partial(_peer_route_kernel, row_base),
        grid=(t // tm,),
        in_specs=[pl.BlockSpec((tm, d), lambda i: (i, 0)),
                  pl.BlockSpec((1, d), lambda i: (0, 0)),
                  pl.BlockSpec((1, 1, d), per_seq),
                  pl.BlockSpec((1, 1, d), per_seq),
                  pl.BlockSpec((d, nq), lambda i: (0, 0)),
                  pl.BlockSpec((2 * PEER_HEADS, N_KEYS, PEER_DQH), lambda i: (0, 0, 0))],
        out_specs=[pl.BlockSpec((tm, d), lambda i: (i, 0)),
                   pl.BlockSpec((tm, PEER_HK), lambda i: (i, 0)),
                   pl.BlockSpec((tm, PEER_HK), lambda i: (i, 0))],
        out_shape=[jax.ShapeDtypeStruct((t, d), F32),
                   jax.ShapeDtypeStruct((t, PEER_HK), jnp.int32),
                   jax.ShapeDtypeStruct((t, PEER_HK), F32)],
        scratch_shapes=[pltpu.VMEM((2 * PEER_HEADS, tm, PEER_DQH), BF16),
                        pltpu.VMEM((PEER_HEADS, PEER_TOPK, tm), jnp.int32),
                        pltpu.VMEM((PEER_HEADS, PEER_TOPK, tm), F32)],
        compiler_params=_cparams("parallel"),
        name="peer_route",
    )(x2, gn.reshape(1, d), sc, sh, wq_bf16, keys_bf16)


SC_ROWS = 16
SC_GROUPS = PEER_HK // SC_ROWS
SC_TB = 32
SC_CHUNKS = D_MODEL // SC_LANES
SC_HALF = D_MODEL // 2
SC_WCH = SC_HALF // SC_LANES
SC_NBUF = 4


def _pack_table(tab):
    b = lax.bitcast_convert_type(tab.reshape(-1, D_MODEL).astype(BF16), jnp.uint16)
    b = b.astype(jnp.uint32)
    return lax.bitcast_convert_type(b[:, :SC_HALF] | (b[:, SC_HALF:] << 16), jnp.int32)


def _unpack_pair(w):
    lo = plsc.bitcast(lax.shift_left(w, jnp.int32(16)), F32)
    hi = plsc.bitcast(w & jnp.int32(-65536), F32)
    return lo, hi


def _pair_products(w, xbf):
    return _unpack_pair(plsc.bitcast(plsc.bitcast(w, BF16) * xbf, jnp.int32))


def _sc_mesh():
    return plsc.VectorSubcoreMesh(core_axis_name="c", subcore_axis_name="s")


def _sc_pipeline(n_items, gather, compute):
    ahead = SC_NBUF - 1
    for i in range(ahead):
        gather(i, i).start()

    @pl.loop(0, n_items, step=SC_NBUF)
    def _(it):
        for b in range(SC_NBUF):
            i = it + b
            gather(i, b).wait()

            @pl.when(i + ahead < n_items)
            def _():
                gather(i + ahead, (b + ahead) % SC_NBUF).start()

            compute(i, b)


def _tree_sum(terms):
    while len(terms) > 1:
        terms = [terms[i] + terms[i + 1] for i in range(0, len(terms), 2)]
    return terms[0]


def _peer_pre(h_pk, idx, u_tab):
    t = h_pk.shape[0]
    info = plsc.get_sparse_core_info()
    nc, nw = info.num_cores, info.num_cores * info.num_subcores
    tpw = t // nw
    n_items = SC_TB * SC_GROUPS

    @functools.partial(
        pl.kernel, mesh=_sc_mesh(),
        out_type=jax.ShapeDtypeStruct((t, PEER_HK), F32),
        scratch_types=[pltpu.VMEM((SC_TB, PEER_HK), jnp.int32),
                       pltpu.VMEM((SC_TB, SC_HALF), jnp.int32),
                       pltpu.VMEM((SC_TB, PEER_HK), F32),
                       pltpu.VMEM((SC_ROWS, SC_LANES), F32)]
                      + [pltpu.VMEM((SC_ROWS, SC_HALF), jnp.int32)] * SC_NBUF
                      + [pltpu.SemaphoreType.DMA] * SC_NBUF,
        compiler_params=pltpu.CompilerParams(needs_layout_passes=False),
        name="peer_pre")
    def k(h_hbm, idx_hbm, u_hbm, p_hbm, idx_v, h_v, p_v, tr_v, *ring):
        wid = lax.axis_index("s") * nc + lax.axis_index("c")
        bufs, sems = ring[:SC_NBUF], ring[SC_NBUF:]
        lane = lax.iota(jnp.int32, SC_LANES)

        def gather(it, b):
            tl, g = it // SC_GROUPS, it % SC_GROUPS
            return pltpu.make_async_copy(
                u_hbm.at[idx_v.at[tl, pl.ds(g * SC_ROWS, SC_ROWS)]], bufs[b], sems[b])

        def compute(it, b):
            tl, g = it // SC_GROUPS, it % SC_GROUPS
            buf = bufs[b]

            def body(c, accs):
                sl = pl.ds(c * SC_LANES, SC_LANES)
                xbf = plsc.bitcast(h_v[tl, sl], BF16)
                out = []
                for r in range(SC_ROWS):
                    lo, hi = _pair_products(buf[r, sl], xbf)
                    out.append(accs[r] + (lo + hi))
                return tuple(out)

            zero = jnp.zeros((SC_LANES,), F32)
            accs = lax.fori_loop(0, SC_WCH, body, (zero,) * SC_ROWS)
            skew = [(lane + j) & (SC_LANES - 1) for j in range(SC_LANES)]
            for r in range(SC_ROWS):
                plsc.store_scatter(tr_v, [jnp.full((SC_LANES,), r, jnp.int32), skew[r]], accs[r])
            cols = [plsc.load_gather(tr_v, [lane, skew[j]]) for j in range(SC_LANES)]
            p_v[tl, pl.ds(g * SC_ROWS, SC_ROWS)] = _tree_sum(cols)

        @pl.loop(0, tpw // SC_TB)
        def _(blk):
            t0 = wid * tpw + blk * SC_TB
            pltpu.sync_copy(idx_hbm.at[pl.ds(t0, SC_TB)], idx_v)
            pltpu.sync_copy(h_hbm.at[pl.ds(t0, SC_TB)], h_v)
            _sc_pipeline(n_items, gather, compute)
            pltpu.sync_copy(p_v, p_hbm.at[pl.ds(t0, SC_TB)])

    return k(h_pk, idx, u_tab)


def _peer_post(act, idx, v_tab):
    t = act.shape[0]
    d = 2 * v_tab.shape[1]
    info = plsc.get_sparse_core_info()
    nc, nw = info.num_cores, info.num_cores * info.num_subcores
    tpw = t // nw
    n_items = SC_TB * SC_GROUPS

    @functools.partial(
        pl.kernel, mesh=_sc_mesh(),
        out_type=jax.ShapeDtypeStruct((t, d), F32),
        scratch_types=[pltpu.VMEM((SC_TB, PEER_HK), jnp.int32),
                       pltpu.VMEM((SC_TB, PEER_HK), jnp.int32),
                       pltpu.VMEM((SC_TB, d), F32)]
                      + [pltpu.VMEM((SC_ROWS, SC_HALF), jnp.int32)] * SC_NBUF
                      + [pltpu.SemaphoreType.DMA] * SC_NBUF,
        compiler_params=pltpu.CompilerParams(needs_layout_passes=False),
        name="peer_post")
    def k(a_hbm, idx_hbm, v_hbm, y_hbm, idx_v, a_v, y_v, *ring):
        wid = lax.axis_index("s") * nc + lax.axis_index("c")
        bufs, sems = ring[:SC_NBUF], ring[SC_NBUF:]

        def gather(it, b):
            tl, g = it // SC_GROUPS, it % SC_GROUPS
            return pltpu.make_async_copy(
                v_hbm.at[idx_v.at[tl, pl.ds(g * SC_ROWS, SC_ROWS)]], bufs[b], sems[b])

        def compute(it, b):
            tl, g = it // SC_GROUPS, it % SC_GROUPS
            buf = bufs[b]
            tlv = jnp.full((SC_LANES,), tl, jnp.int32)
            w = tuple(plsc.bitcast(plsc.load_gather(
                a_v, [tlv, jnp.full((SC_LANES,), g * SC_ROWS + r, jnp.int32)]), BF16)
                for r in range(SC_ROWS))

            @plsc.parallel_loop(0, SC_WCH, unroll=2)
            def _(c):
                sl = pl.ds(c * SC_LANES, SC_LANES)
                parts = [_pair_products(buf[r, sl], w[r]) for r in range(SC_ROWS)]
                plsc.addupdate(y_v.at[tl, sl], _tree_sum([p[0] for p in parts]))
                plsc.addupdate(y_v.at[tl, pl.ds(SC_HALF + c * SC_LANES, SC_LANES)],
                               _tree_sum([p[1] for p in parts]))

        @pl.loop(0, tpw // SC_TB)
        def _(blk):
            t0 = wid * tpw + blk * SC_TB
            pltpu.sync_copy(idx_hbm.at[pl.ds(t0, SC_TB)], idx_v)
            pltpu.sync_copy(a_hbm.at[pl.ds(t0, SC_TB)], a_v)

            @pl.loop(0, SC_TB)
            def _(r):
                @plsc.parallel_loop(0, SC_CHUNKS, unroll=4)
                def _(c):
                    y_v[r, pl.ds(c * SC_LANES, SC_LANES)] = jnp.zeros((SC_LANES,), F32)

            _sc_pipeline(n_items, gather, compute)
            pltpu.sync_copy(y_v, y_hbm.at[pl.ds(t0, SC_TB)])

    return k(act, idx, v_tab)


def _act_kernel(p_ref, g_ref, o_ref):
    p = p_ref[...]
    o_ref[...] = (p * (lax.erf(p * (1.0 / math.sqrt(2.0))) + 1.0) * 0.5) * g_ref[...]


def _expert_act(p, gates, tm=512):
    t, n = p.shape
    spec = pl.BlockSpec((tm, n), lambda i: (i, 0))
    return pl.pallas_call(
        _act_kernel, grid=(t // tm,), in_specs=[spec, spec], out_specs=spec,
        out_shape=jax.ShapeDtypeStruct((t, n), F32),
        compiler_params=_cparams("parallel"), name="expert_act",
    )(p, gates)


def _res_kernel(x_ref, y_ref, g_ref, o_ref):
    o_ref[...] = x_ref[...] + g_ref[0] * y_ref[...]


def _res_norm_kernel(x_ref, y_ref, g_ref, fg_ref, o_ref):
    x = x_ref[...] + g_ref[0] * y_ref[...]
    ms = jnp.mean(x * x, axis=-1, keepdims=True)
    o_ref[...] = x * lax.rsqrt(ms + EPS) * fg_ref[...]


def _residual(x2, y2, g, seq, final_g=None, tm=512):
    t, d = x2.shape
    tile = pl.BlockSpec((tm, d), lambda i: (i, 0))
    gspec = pl.BlockSpec((1, 1, d), lambda i: ((i * tm) // seq, 0, 0))
    if final_g is None:
        kern, ins, args = _res_kernel, [tile, tile, gspec], (x2, y2, g)
    else:
        kern = _res_norm_kernel
        ins = [tile, tile, gspec, pl.BlockSpec((1, d), lambda i: (0, 0))]
        args = (x2, y2, g, final_g.reshape(1, d))
    return pl.pallas_call(
        kern, grid=(t // tm,), in_specs=ins, out_specs=tile,
        out_shape=jax.ShapeDtypeStruct((t, d), F32),
        compiler_params=_cparams("parallel"), name="residual",
    )(*args)


def _pack_same(a):
    b = lax.bitcast_convert_type(a.astype(BF16), jnp.uint16).astype(jnp.uint32)
    return lax.bitcast_convert_type(b | (b << 16), jnp.int32)


def _peer(x2, seq, gn, sc, sh, wq_bf16, keys_bf16, u_all, v_all, layer):
    rows = _peer_route(x2, seq, gn, sc, sh, wq_bf16, keys_bf16, layer * N_KEYS * N_KEYS)
    h2, experts, gates = rows
    p = _peer_pre(_pack_table(h2), experts, u_all)
    act = _expert_act(p, gates)
    return _peer_post(_pack_same(act), experts, v_all)


BATCH_SPLIT = 8


def kernel(x, c, ada_w, ada_b, norm_mix_g, norm_ffn_g, cv_w_in, cv_b_in, cv_w_dw, cv_b_dw, cv_ln_g, cv_ln_b, cv_w_out, cv_b_out, sb_w_qkv, sb_w_o, pk_w_q, pk_keys, pk_u, pk_v, final_g):
    bsz, seq, d = x.shape
    mod = _ada_mod(c, ada_w, ada_b)
    w_in, w_out = cv_w_in.astype(BF16), cv_w_out.astype(BF16)
    w_qkv, w_o = sb_w_qkv.astype(BF16), sb_w_o.astype(BF16)
    w_q = pk_w_q.astype(BF16)
    keys = pk_keys.reshape(DEPTH, 2 * PEER_HEADS, N_KEYS, PEER_DQH).astype(BF16)
    u_all, v_all = _pack_table(pk_u), _pack_table(pk_v)
    nb = bsz // BATCH_SPLIT
    t = nb * seq
    outs = []
    for part in range(BATCH_SPLIT):
        lo = part * nb
        x2 = x[lo:lo + nb].reshape(t, d)
        for i in range(DEPTH):
            sh1, sc1, g1, sh2, sc2, g2 = (
                mod[i, lo:lo + nb, n * d:(n + 1) * d].reshape(nb, 1, d) for n in range(ADA_CHUNKS))
            j = i // 2
            if i % 2 == 0:
                u2 = _norm_glu(x2, seq, norm_mix_g[i], sc1, sh1, w_in[j], cv_b_in[j])
                x2 = _conv_block(u2.reshape(nb, seq, d), x2.reshape(nb, seq, d), g1,
                                 cv_w_dw[j], cv_b_dw[j], cv_ln_g[j], cv_ln_b[j],
                                 w_out[j], cv_b_out[j]).reshape(t, d)
            else:
                qkv = _norm_mm(x2, seq, norm_mix_g[i], sc1, sh1, w_qkv[j], BF16)
                o = _attention(qkv.reshape(nb, seq, 3 * d))
                x2 = _mm_res(o.reshape(t, d), w_o[j], x2, g1, seq)
            y2 = _peer(x2, seq, norm_ffn_g[i], sc2, sh2, w_q[i], keys[i], u_all, v_all, i)
            x2 = _residual(x2, y2, g2, seq, final_g if i == DEPTH - 1 else None)
        outs.append(x2.reshape(nb, seq, d))
    return jnp.concatenate(outs, axis=0)
```

```python
import functools
import math

import jax
import jax.numpy as jnp
from jax import lax
from jax.experimental import pallas as pl
from jax.experimental.pallas import tpu as pltpu
from jax.experimental.pallas import tpu_sc as plsc

D_MODEL = 1024
DEPTH = 2
CONV_WIDTH = 31
N_HEADS = 16
HEAD_DIM = 64
PEER_HEADS = 8
PEER_DQH = 128
N_KEYS = 128
PEER_TOPK = 16
PEER_HK = PEER_HEADS * PEER_TOPK
ADA_CHUNKS = 6
EPS = 1e-6

LANES = 128
SUBLANES = 8
SC_LANES = 16
VMEM_LIMIT = 48 * 1024 * 1024

F32 = jnp.float32
BF16 = jnp.bfloat16
NEG_INF = float("-inf")


def _cparams(*sem):
    return pltpu.CompilerParams(dimension_semantics=sem, vmem_limit_bytes=VMEM_LIMIT)


def _norm_mod(x, gn, sc, sh):
    ms = jnp.mean(x * x, axis=-1, keepdims=True)
    return (x * lax.rsqrt(ms + EPS)) * gn * (1.0 + sc) + sh


def _ada_kernel(c_ref, w_ref, b_ref, o_ref):
    c = c_ref[...]
    ca = c * jax.nn.sigmoid(c)
    o_ref[0] = jnp.dot(ca.astype(BF16), w_ref[0].astype(BF16),
                       preferred_element_type=F32) + b_ref[0]


def _ada_mod(c, ada_w, ada_b):
    depth, d, n = ada_w.shape
    bsz = c.shape[0]
    tn = 512
    return pl.pallas_call(
        _ada_kernel,
        grid=(depth, n // tn),
        in_specs=[pl.BlockSpec((bsz, d), lambda i, j: (0, 0)),
                  pl.BlockSpec((1, d, tn), lambda i, j: (i, 0, j)),
                  pl.BlockSpec((1, 1, tn), lambda i, j: (i, 0, j))],
        out_specs=pl.BlockSpec((1, bsz, tn), lambda i, j: (i, 0, j)),
        out_shape=jax.ShapeDtypeStruct((depth, bsz, n), F32),
        compiler_params=_cparams("parallel", "parallel"),
        name="ada_mod",
    )(c, ada_w, ada_b.reshape(depth, 1, n))


def _norm_mm_kernel(x_ref, gn_ref, sc_ref, sh_ref, w_ref, o_ref):
    h = _norm_mod(x_ref[...], gn_ref[...], sc_ref[0], sh_ref[0])
    o_ref[...] = jnp.dot(h.astype(BF16), w_ref[...],
                         preferred_element_type=F32).astype(o_ref.dtype)


def _norm_mm(x2, seq, gn, sc, sh, w_bf16, out_dtype, tm=256):
    t, d = x2.shape
    n = w_bf16.shape[1]
    per_seq = lambda i: ((i * tm) // seq, 0, 0)
    return pl.pallas_call(
        _norm_mm_kernel,
        grid=(t // tm,),
        in_specs=[pl.BlockSpec((tm, d), lambda i: (i, 0)),
                  pl.BlockSpec((1, d), lambda i: (0, 0)),
                  pl.BlockSpec((1, 1, d), per_seq),
                  pl.BlockSpec((1, 1, d), per_seq),
                  pl.BlockSpec((d, n), lambda i: (0, 0))],
        out_specs=pl.BlockSpec((tm, n), lambda i: (i, 0)),
        out_shape=jax.ShapeDtypeStruct((t, n), out_dtype),
        compiler_params=_cparams("parallel"),
        name="norm_mm",
    )(x2, gn.reshape(1, d), sc, sh, w_bf16)


def _norm_glu_kernel(x_ref, gn_ref, sc_ref, sh_ref, wa_ref, wg_ref, ba_ref, bg_ref, o_ref):
    h = _norm_mod(x_ref[...], gn_ref[...], sc_ref[0], sh_ref[0]).astype(BF16)
    a = jnp.dot(h, wa_ref[...], preferred_element_type=F32) + ba_ref[...]
    g = jnp.dot(h, wg_ref[...], preferred_element_type=F32) + bg_ref[...]
    o_ref[...] = a * jax.nn.sigmoid(g)


def _norm_glu(x2, seq, gn, sc, sh, w_in_bf16, b_in, seq0, nseq, tm=256):
    d = x2.shape[1]
    t = nseq * seq
    n = w_in_bf16.shape[1] // 2
    blk0 = seq0 * seq // tm
    per_seq = lambda i: ((i * tm) // seq + seq0, 0, 0)
    b2 = b_in.reshape(1, 2 * n)
    return pl.pallas_call(
        _norm_glu_kernel,
        grid=(t // tm,),
        in_specs=[pl.BlockSpec((tm, d), lambda i: (i + blk0, 0)),
                  pl.BlockSpec((1, d), lambda i: (0, 0)),
                  pl.BlockSpec((1, 1, d), per_seq),
                  pl.BlockSpec((1, 1, d), per_seq),
                  pl.BlockSpec((d, n), lambda i: (0, 0)),
                  pl.BlockSpec((d, n), lambda i: (0, 1)),
                  pl.BlockSpec((1, n), lambda i: (0, 0)),
                  pl.BlockSpec((1, n), lambda i: (0, 1))],
        out_specs=pl.BlockSpec((tm, n), lambda i: (i, 0)),
        out_shape=jax.ShapeDtypeStruct((t, n), F32),
        compiler_params=_cparams("parallel"),
        name="norm_glu",
    )(x2, gn.reshape(1, d), sc, sh, w_in_bf16, w_in_bf16, b2, b2)


HALO = 32


def _conv_kernel(cur_ref, prev_ref, wdw_ref, bdw_ref, lng_ref, lnb_ref, wout_ref,
                 bout_ref, x_ref, g_ref, o_ref, win_s):
    ts = cur_ref.shape[1]
    first = pl.program_id(1) == 0
    win_s[0:HALO, :] = jnp.where(first, 0.0, prev_ref[0])
    win_s[HALO:HALO + ts, :] = cur_ref[0]
    acc = jnp.zeros(cur_ref.shape[1:], F32) + bdw_ref[...]
    off = HALO - (CONV_WIDTH - 1)
    for k in range(CONV_WIDTH):
        acc = acc + win_s[off + k:off + k + ts, :] * wdw_ref[k:k + 1, :]
    mu = jnp.mean(acc, axis=-1, keepdims=True)
    cen = acc - mu
    var = jnp.mean(cen * cen, axis=-1, keepdims=True)
    y = cen * lax.rsqrt(var + EPS) * lng_ref[...] + lnb_ref[...]
    y = y * jax.nn.sigmoid(y)
    m = jnp.dot(y.astype(BF16), wout_ref[...], preferred_element_type=F32) + bout_ref[...]
    o_ref[0] = x_ref[0] + g_ref[0] * m


def _conv_block(u3, x3, g1, w_dw, b_dw, ln_g, ln_b, w_out_bf16, b_out, seq0, ts=256):
    bsz, seq, d = u3.shape
    r = ts // HALO
    wdw = jnp.concatenate([w_dw, jnp.zeros((HALO - CONV_WIDTH, d), F32)], axis=0)
    vec = lambda b, i: (0, 0)
    return pl.pallas_call(
        _conv_kernel,
        grid=(bsz, seq // ts),
        in_specs=[pl.BlockSpec((1, ts, d), lambda b, i: (b, i, 0)),
                  pl.BlockSpec((1, HALO, d), lambda b, i: (b, jnp.maximum(i * r - 1, 0), 0)),
                  pl.BlockSpec((HALO, d), vec),
                  pl.BlockSpec((1, d), vec),
                  pl.BlockSpec((1, d), vec),
                  pl.BlockSpec((1, d), vec),
                  pl.BlockSpec((d, d), vec),
                  pl.BlockSpec((1, d), vec),
                  pl.BlockSpec((1, ts, d), lambda b, i: (b + seq0, i, 0)),
                  pl.BlockSpec((1, 1, d), lambda b, i: (b + seq0, 0, 0))],
        out_specs=pl.BlockSpec((1, ts, d), lambda b, i: (b, i, 0)),
        out_shape=jax.ShapeDtypeStruct((bsz, seq, d), F32),
        scratch_shapes=[pltpu.VMEM((HALO + ts, d), F32)],
        compiler_params=_cparams("parallel", "arbitrary"),
        name="conv_block",
    )(u3, u3, wdw, b_dw.reshape(1, d), ln_g.reshape(1, d), ln_b.reshape(1, d),
      w_out_bf16, b_out.reshape(1, d), x3, g1)


ATT_T = 256


def _attn_kernel(q_ref, k_ref, v_ref, o_ref):
    t = ATT_T
    qi = pl.program_id(2)
    q2 = q_ref[0] * (1.0 / math.sqrt(HEAD_DIM))
    lane = lax.broadcasted_iota(jnp.int32, (1, LANES), 1)
    head_mask = (lane < HEAD_DIM, lane >= HEAD_DIM)
    row = lax.broadcasted_iota(jnp.int32, (t, t), 0)
    col = lax.broadcasted_iota(jnp.int32, (t, t), 1)
    tri = (row >= col).astype(BF16)
    tri2 = jnp.concatenate([tri, tri], axis=0)
    qh = tuple(jnp.where(m, q2, jnp.zeros_like(q2)) for m in head_mask)
    nt = (((1,), (1,)), ((), ()))

    def tile(kt, suf, acc, diag):
        start = pl.multiple_of(kt * t, t)
        k2 = k_ref[0, pl.ds(start, t), :]
        v2 = v_ref[0, pl.ds(start, t), :]
        new_suf = []
        for h in range(2):
            z = lax.dot_general(qh[h], k2, nt, preferred_element_type=F32)
            sp = jnp.maximum(z, 0.0) + jnp.log(1.0 + jnp.exp(-jnp.abs(z)))
            if diag:
                sp = jnp.where(col < row, sp, 0.0)
            hi = sp.astype(BF16)
            mid = (sp - hi.astype(F32)).astype(BF16)
            cs = jnp.dot(jnp.concatenate([hi, mid], axis=1), tri2,
                         preferred_element_type=F32)
            a = jnp.exp(z - cs - suf[h])
            if diag:
                a = jnp.where(col < row, a, 0.0)
            pv = jnp.dot(a.astype(BF16), v2, preferred_element_type=F32)
            acc = acc + jnp.where(head_mask[h], pv, 0.0)
            new_suf.append(suf[h] + cs[:, 0:1])
        return tuple(new_suf), acc

    zero = jnp.zeros((t, 1), F32)
    suf, acc = tile(qi, (zero, zero), jnp.zeros((t, LANES), F32), True)

    def body(n, carry):
        s0, s1, acc = carry
        (s0, s1), acc = tile(qi - n, (s0, s1), acc, False)
        return s0, s1, acc

    _, _, acc = lax.fori_loop(1, qi + 1, body, (suf[0], suf[1], acc))
    o_ref[0] = acc


def _attention(qkv3):
    bsz, seq, n3 = qkv3.shape
    d = n3 // 3
    nb = d // LANES
    return pl.pallas_call(
        _attn_kernel,
        grid=(bsz, nb, seq // ATT_T),
        in_specs=[pl.BlockSpec((1, ATT_T, LANES), lambda b, h, i: (b, i, h)),
                  pl.BlockSpec((1, seq, LANES), lambda b, h, i: (b, 0, nb + h)),
                  pl.BlockSpec((1, seq, LANES), lambda b, h, i: (b, 0, 2 * nb + h))],
        out_specs=pl.BlockSpec((1, ATT_T, LANES), lambda b, h, i: (b, i, h)),
        out_shape=jax.ShapeDtypeStruct((bsz, seq, d), F32),
        compiler_params=_cparams("parallel", "parallel", "arbitrary"),
        name="sb_attention",
    )(qkv3, qkv3, qkv3)


def _mm_res_kernel(a_ref, w_ref, x_ref, g_ref, o_ref):
    m = jnp.dot(a_ref[...].astype(BF16), w_ref[...], preferred_element_type=F32)
    o_ref[...] = x_ref[...] + g_ref[0] * m


def _mm_res(a2, w_bf16, x2, g, seq, tm=512):
    t, d = x2.shape
    k = a2.shape[1]
    return pl.pallas_call(
        _mm_res_kernel,
        grid=(t // tm,),
        in_specs=[pl.BlockSpec((tm, k), lambda i: (i, 0)),
                  pl.BlockSpec((k, d), lambda i: (0, 0)),
                  pl.BlockSpec((tm, d), lambda i: (i, 0)),
                  pl.BlockSpec((1, 1, d), lambda i: ((i * tm) // seq, 0, 0))],
        out_specs=pl.BlockSpec((tm, d), lambda i: (i, 0)),
        out_shape=jax.ShapeDtypeStruct((t, d), F32),
        compiler_params=_cparams("parallel"),
        name="mm_res",
    )(a2, w_bf16, x2, g)


PAIR_ROWS = tuple(PEER_TOPK // (i + 1) for i in range(PEER_TOPK))


def _extract_top(s, payload, k):
    rows = lax.broadcasted_iota(jnp.int32, s.shape, 0)
    big = s.shape[0]
    vals, outs = [], []
    for _ in range(k):
        m = jnp.max(s, axis=0, keepdims=True)
        r = jnp.min(jnp.where(s == m, rows, big), axis=0, keepdims=True)
        sel = rows == r
        vals.append(m)
        if payload is None:
            outs.append(r)
        else:
            outs.append(jnp.max(jnp.where(sel, payload, -1), axis=0, keepdims=True))
        s = jnp.where(sel, NEG_INF, s)
    return jnp.concatenate(vals, axis=0), jnp.concatenate(outs, axis=0)


def _pair_candidates(a, ia, b, ib):
    sub = lax.broadcasted_iota(jnp.int32, (SUBLANES, a.shape[1]), 0)
    cand, ids = [], []
    for i in range(SUBLANES):
        for j0 in range(0, PAIR_ROWS[i], SUBLANES):
            n = min(PAIR_ROWS[i] - j0, SUBLANES)
            c = a[i:i + 1, :] + b[j0:j0 + SUBLANES, :]
            e = ia[i:i + 1, :] * N_KEYS + ib[j0:j0 + SUBLANES, :]
            if n < SUBLANES:
                c = jnp.where(sub < n, c, NEG_INF)
            cand.append(c)
            ids.append(e)
    cand.append(a[SUBLANES:, :] + b[0:1, :])
    ids.append(ia[SUBLANES:, :] * N_KEYS + ib[0:1, :])
    return jnp.concatenate(cand, axis=0), jnp.concatenate(ids, axis=0)


def _peer_route_kernel(row_base, x_ref, gn_ref, sc_ref, sh_ref, wq_ref, keys_ref,
                       h_ref, e_ref, g_ref, q_s, e_s, g_s):
    h = _norm_mod(x_ref[...], gn_ref[...], sc_ref[0], sh_ref[0])
    h_ref[...] = h
    q = jnp.dot(h.astype(BF16), wq_ref[...], preferred_element_type=F32)
    for j in range(2 * PEER_HEADS):
        q_s[j] = q[:, j * PEER_DQH:(j + 1) * PEER_DQH].astype(BF16)
    nt = (((1,), (1,)), ((), ()))

    def head(hd, carry):
        tops = []
        for c in range(2):
            s_t = lax.dot_general(keys_ref[2 * hd + c], q_s[2 * hd + c], nt,
                                  preferred_element_type=F32)
            tops.append(_extract_top(s_t, None, PEER_TOPK))
        cand, ids = _pair_candidates(tops[0][0], tops[0][1], tops[1][0], tops[1][1])
        best, experts = _extract_top(cand, ids, PEER_TOPK)
        ex = jnp.exp(best - best[0:1, :])
        g_s[hd] = ex / jnp.sum(ex, axis=0, keepdims=True)
        e_s[hd] = experts
        return carry

    lax.fori_loop(0, PEER_HEADS, head, 0)
    tm = x_ref.shape[0]
    e_ref[...] = e_s[...].reshape(PEER_HK, tm).T + row_base
    g_ref[...] = g_s[...].reshape(PEER_HK, tm).T


def _peer_route(x2, seq, gn, sc, sh, wq_bf16, keys_bf16, row_base, tm=256):
    t, d = x2.shape
    nq = wq_bf16.shape[1]
    per_seq = lambda i: ((i * tm) // seq, 0, 0)
    return pl.pallas_call(
        functools.partial(_peer_route_kernel, row_base),
        grid=(t // tm,),
        in_specs=[pl.BlockSpec((tm, d), lambda i: (i, 0)),
                  pl.BlockSpec((1, d), lambda i: (0, 0)),
                  pl.BlockSpec((1, 1, d), per_seq),
                  pl.BlockSpec((1, 1, d), per_seq),
                  pl.BlockSpec((d, nq), lambda i: (0, 0)),
                  pl.BlockSpec((2 * PEER_HEADS, N_KEYS, PEER_DQH), lambda i: (0, 0, 0))],
        out_specs=[pl.BlockSpec((tm, d), lambda i: (i, 0)),
                   pl.BlockSpec((tm, PEER_HK), lambda i: (i, 0)),
                   pl.BlockSpec((tm, PEER_HK), lambda i: (i, 0))],
        out_shape=[jax.ShapeDtypeStruct((t, d), F32),
                   jax.ShapeDtypeStruct((t, PEER_HK), jnp.int32),
                   jax.ShapeDtypeStruct((t, PEER_HK), F32)],
        scratch_shapes=[pltpu.VMEM((2 * PEER_HEADS, tm, PEER_DQH), BF16),
                        pltpu.VMEM((PEER_HEADS, PEER_TOPK, tm), jnp.int32),
                        pltpu.VMEM((PEER_HEADS, PEER_TOPK, tm), F32)],
        compiler_params=_cparams("parallel"),
        name="peer_route",
    )(x2, gn.reshape(1, d), sc, sh, wq_bf16, keys_bf16)


SC_ROWS = 16
SC_GROUPS = PEER_HK // SC_ROWS
SC_TB = 32
SC_CHUNKS = D_MODEL // SC_LANES
SC_HALF = D_MODEL // 2
SC_WCH = SC_HALF // SC_LANES
SC_NBUF = 4


def _pack_table(tab):
    b = lax.bitcast_convert_type(tab.reshape(-1, D_MODEL).astype(BF16), jnp.uint16)
    b = b.astype(jnp.uint32)
    return lax.bitcast_convert_type(b[:, :SC_HALF] | (b[:, SC_HALF:] << 16), jnp.int32)


def _unpack_pair(w):
    lo = plsc.bitcast(lax.shift_left(w, jnp.int32(16)), F32)
    hi = plsc.bitcast(w & jnp.int32(-65536), F32)
    return lo, hi


def _pair_products(w, xbf):
    return _unpack_pair(plsc.bitcast(plsc.bitcast(w, BF16) * xbf, jnp.int32))


def _sc_mesh():
    return plsc.VectorSubcoreMesh(core_axis_name="c", subcore_axis_name="s")


def _sc_pipeline(n_items, gather, compute):
    ahead = SC_NBUF - 1
    for i in range(ahead):
        gather(i, i).start()

    @pl.loop(0, n_items, step=SC_NBUF)
    def _(it):
        for b in range(SC_NBUF):
            i = it + b
            gather(i, b).wait()

            @pl.when(i + ahead < n_items)
            def _():
                gather(i + ahead, (b + ahead) % SC_NBUF).start()

            compute(i, b)


def _tree_sum(terms):
    while len(terms) > 1:
        terms = [terms[i] + terms[i + 1] for i in range(0, len(terms), 2)]
    return terms[0]


def _peer_pre(h_pk, idx, u_tab):
    t = h_pk.shape[0]
    info = plsc.get_sparse_core_info()
    nc, nw = info.num_cores, info.num_cores * info.num_subcores
    tpw = t // nw
    n_items = SC_TB * SC_GROUPS

    @functools.partial(
        pl.kernel, mesh=_sc_mesh(),
        out_type=jax.ShapeDtypeStruct((t, PEER_HK), F32),
        scratch_types=[pltpu.VMEM((SC_TB, PEER_HK), jnp.int32),
                       pltpu.VMEM((SC_TB, SC_HALF), jnp.int32),
                       pltpu.VMEM((SC_TB, PEER_HK), F32),
                       pltpu.VMEM((SC_ROWS, SC_LANES), F32)]
                      + [pltpu.VMEM((SC_ROWS, SC_HALF), jnp.int32)] * SC_NBUF
                      + [pltpu.SemaphoreType.DMA] * SC_NBUF,
        compiler_params=pltpu.CompilerParams(needs_layout_passes=False),
        name="peer_pre")
    def k(h_hbm, idx_hbm, u_hbm, p_hbm, idx_v, h_v, p_v, tr_v, *ring):
        wid = lax.axis_index("s") * nc + lax.axis_index("c")
        bufs, sems = ring[:SC_NBUF], ring[SC_NBUF:]
        lane = lax.iota(jnp.int32, SC_LANES)

        def gather(it, b):
            tl, g = it // SC_GROUPS, it % SC_GROUPS
            return pltpu.make_async_copy(
                u_hbm.at[idx_v.at[tl, pl.ds(g * SC_ROWS, SC_ROWS)]], bufs[b], sems[b])

        def compute(it, b):
            tl, g = it // SC_GROUPS, it % SC_GROUPS
            buf = bufs[b]

            def body(c, accs):
                sl = pl.ds(c * SC_LANES, SC_LANES)
                xbf = plsc.bitcast(h_v[tl, sl], BF16)
                out = []
                for r in range(SC_ROWS):
                    lo, hi = _pair_products(buf[r, sl], xbf)
                    out.append(accs[r] + (lo + hi))
                return tuple(out)

            zero = jnp.zeros((SC_LANES,), F32)
            accs = lax.fori_loop(0, SC_WCH, body, (zero,) * SC_ROWS)
            skew = [(lane + j) & (SC_LANES - 1) for j in range(SC_LANES)]
            for r in range(SC_ROWS):
                plsc.store_scatter(tr_v, [jnp.full((SC_LANES,), r, jnp.int32), skew[r]], accs[r])
            cols = [plsc.load_gather(tr_v, [lane, skew[j]]) for j in range(SC_LANES)]
            p_v[tl, pl.ds(g * SC_ROWS, SC_ROWS)] = _tree_sum(cols)

        @pl.loop(0, tpw // SC_TB)
        def _(blk):
            t0 = wid * tpw + blk * SC_TB
            pltpu.sync_copy(idx_hbm.at[pl.ds(t0, SC_TB)], idx_v)
            pltpu.sync_copy(h_hbm.at[pl.ds(t0, SC_TB)], h_v)
            _sc_pipeline(n_items, gather, compute)
            pltpu.sync_copy(p_v, p_hbm.at[pl.ds(t0, SC_TB)])

    return k(h_pk, idx, u_tab)


def _peer_post(act, idx, v_tab):
    t = act.shape[0]
    d = 2 * v_tab.shape[1]
    info = plsc.get_sparse_core_info()
    nc, nw = info.num_cores, info.num_cores * info.num_subcores
    tpw = t // nw
    n_items = SC_TB * SC_GROUPS

    @functools.partial(
        pl.kernel, mesh=_sc_mesh(),
        out_type=jax.ShapeDtypeStruct((t, d), F32),
        scratch_types=[pltpu.VMEM((SC_TB, PEER_HK), jnp.int32),
                       pltpu.VMEM((SC_TB, PEER_HK), jnp.int32),
                       pltpu.VMEM((SC_TB, d), F32)]
                      + [pltpu.VMEM((SC_ROWS, SC_HALF), jnp.int32)] * SC_NBUF
                      + [pltpu.SemaphoreType.DMA] * SC_NBUF,
        compiler_params=pltpu.CompilerParams(needs_layout_passes=False),
        name="peer_post")
    def k(a_hbm, idx_hbm, v_hbm, y_hbm, idx_v, a_v, y_v, *ring):
        wid = lax.axis_index("s") * nc + lax.axis_index("c")
        bufs, sems = ring[:SC_NBUF], ring[SC_NBUF:]

        def gather(it, b):
            tl, g = it // SC_GROUPS, it % SC_GROUPS
            return pltpu.make_async_copy(
                v_hbm.at[idx_v.at[tl, pl.ds(g * SC_ROWS, SC_ROWS)]], bufs[b], sems[b])

        def compute(it, b):
            tl, g = it // SC_GROUPS, it % SC_GROUPS
            buf = bufs[b]
            tlv = jnp.full((SC_LANES,), tl, jnp.int32)
            w = tuple(plsc.bitcast(plsc.load_gather(
                a_v, [tlv, jnp.full((SC_LANES,), g * SC_ROWS + r, jnp.int32)]), BF16)
                for r in range(SC_ROWS))

            @plsc.parallel_loop(0, SC_WCH, unroll=2)
            def _(c):
                sl = pl.ds(c * SC_LANES, SC_LANES)
                parts = [_pair_products(buf[r, sl], w[r]) for r in range(SC_ROWS)]
                plsc.addupdate(y_v.at[tl, sl], _tree_sum([p[0] for p in parts]))
                plsc.addupdate(y_v.at[tl, pl.ds(SC_HALF + c * SC_LANES, SC_LANES)],
                               _tree_sum([p[1] for p in parts]))

        @pl.loop(0, tpw // SC_TB)
        def _(blk):
            t0 = wid * tpw + blk * SC_TB
            pltpu.sync_copy(idx_hbm.at[pl.ds(t0, SC_TB)], idx_v)
            pltpu.sync_copy(a_hbm.at[pl.ds(t0, SC_TB)], a_v)

            @pl.loop(0, SC_TB)
            def _(r):
                @plsc.parallel_loop(0, SC_CHUNKS, unroll=4)
                def _(c):
                    y_v[r, pl.ds(c * SC_LANES, SC_LANES)] = jnp.zeros((SC_LANES,), F32)

            _sc_pipeline(n_items, gather, compute)
            pltpu.sync_copy(y_v, y_hbm.at[pl.ds(t0, SC_TB)])

    return k(act, idx, v_tab)


def _act_kernel(p_ref, g_ref, o_ref):
    p = p_ref[...]
    o_ref[...] = (p * (lax.erf(p * (1.0 / math.sqrt(2.0))) + 1.0) * 0.5) * g_ref[...]


def _expert_act(p, gates, tm=512):
    t, n = p.shape
    spec = pl.BlockSpec((tm, n), lambda i: (i, 0))
    return pl.pallas_call(
        _act_kernel, grid=(t // tm,), in_specs=[spec, spec], out_specs=spec,
        out_shape=jax.ShapeDtypeStruct((t, n), F32),
        compiler_params=_cparams("parallel"), name="expert_act",
    )(p, gates)


def _res_kernel(x_ref, y_ref, g_ref, o_ref):
    o_ref[...] = x_ref[...] + g_ref[0] * y_ref[...]


def _res_norm_kernel(x_ref, y_ref, g_ref, fg_ref, o_ref):
    x = x_ref[...] + g_ref[0] * y_ref[...]
    ms = jnp.mean(x * x, axis=-1, keepdims=True)
    o_ref[...] = x * lax.rsqrt(ms + EPS) * fg_ref[...]


def _residual(x2, y2, g, seq, final_g=None, tm=512):
    t, d = x2.shape
    tile = pl.BlockSpec((tm, d), lambda i: (i, 0))
    gspec = pl.BlockSpec((1, 1, d), lambda i: ((i * tm) // seq, 0, 0))
    if final_g is None:
        kern, ins, args = _res_kernel, [tile, tile, gspec], (x2, y2, g)
    else:
        kern = _res_norm_kernel
        ins = [tile, tile, gspec, pl.BlockSpec((1, d), lambda i: (0, 0))]
        args = (x2, y2, g, final_g.reshape(1, d))
    return pl.pallas_call(
        kern, grid=(t // tm,), in_specs=ins, out_specs=tile,
        out_shape=jax.ShapeDtypeStruct((t, d), F32),
        compiler_params=_cparams("parallel"), name="residual",
    )(*args)


def _pack_same(a):
    b = lax.bitcast_convert_type(a.astype(BF16), jnp.uint16).astype(jnp.uint32)
    return lax.bitcast_convert_type(b | (b << 16), jnp.int32)


def _peer(x2, seq, gn, sc, sh, wq_bf16, keys_bf16, u_tab, v_tab):
    h2, experts, gates = _peer_route(x2, seq, gn, sc, sh, wq_bf16, keys_bf16, 0)
    p = _peer_pre(_pack_table(h2), experts, u_tab)
    act = _expert_act(p, gates)
    return _peer_post(_pack_same(act), experts, v_tab)


BATCH_SPLIT = 16


def kernel(x, c, ada_w, ada_b, norm_mix_g, norm_ffn_g, cv_w_in, cv_b_in, cv_w_dw, cv_b_dw, cv_ln_g, cv_ln_b, cv_w_out, cv_b_out, sb_w_qkv, sb_w_o, pk_w_q, pk_keys, pk_u, pk_v, final_g):
    bsz, seq, d = x.shape
    mod = _ada_mod(c, ada_w, ada_b)
    w_in, w_out = cv_w_in.astype(BF16), cv_w_out.astype(BF16)
    w_qkv, w_o = sb_w_qkv.astype(BF16), sb_w_o.astype(BF16)
    w_q = pk_w_q.astype(BF16)
    keys = pk_keys.reshape(DEPTH, 2 * PEER_HEADS, N_KEYS, PEER_DQH).astype(BF16)
    u_tabs = [_pack_table(pk_u[i]) for i in range(DEPTH)]
    v_tabs = [_pack_table(pk_v[i]) for i in range(DEPTH)]
    nb = bsz // BATCH_SPLIT
    t = nb * seq
    outs = []
    for part in range(BATCH_SPLIT):
        lo = part * nb
        x2 = None
        for i in range(DEPTH):
            full = [mod[i, :, n * d:(n + 1) * d].reshape(bsz, 1, d) for n in range(ADA_CHUNKS)]
            sh1, sc1, g1, sh2, sc2, g2 = (m[lo:lo + nb] for m in full)
            j = i // 2
            if i % 2 == 0:
                if i == 0:
                    xin, scv, shv, gv, seq0 = x, full[1], full[0], full[2], lo
                else:
                    xin, scv, shv, gv, seq0 = x2.reshape(nb, seq, d), sc1, sh1, g1, 0
                u2 = _norm_glu(xin.reshape(-1, d), seq, norm_mix_g[i], scv, shv,
                               w_in[j], cv_b_in[j], seq0, nb)
                x2 = _conv_block(u2.reshape(nb, seq, d), xin, gv,
                                 cv_w_dw[j], cv_b_dw[j], cv_ln_g[j], cv_ln_b[j],
                                 w_out[j], cv_b_out[j], seq0).reshape(t, d)
            else:
                qkv = _norm_mm(x2, seq, norm_mix_g[i], sc1, sh1, w_qkv[j], BF16)
                o = _attention(qkv.reshape(nb, seq, 3 * d))
                x2 = _mm_res(o.reshape(t, d), w_o[j], x2, g1, seq)
            y2 = _peer(x2, seq, norm_ffn_g[i], sc2, sh2, w_q[i], keys[i], u_tabs[i], v_tabs[i])
            x2 = _residual(x2, y2, g2, seq, final_g if i == DEPTH - 1 else None)
        outs.append(x2.reshape(nb, seq, d))
    return jnp.concatenate(outs, axis=0)
```

```python
import functools
import math

import jax
import jax.numpy as jnp
from jax import lax
from jax.experimental import pallas as pl
from jax.experimental.pallas import tpu as pltpu
from jax.experimental.pallas import tpu_sc as plsc

D_MODEL = 1024
DEPTH = 2
CONV_WIDTH = 31
N_HEADS = 16
HEAD_DIM = 64
PEER_HEADS = 8
PEER_DQH = 128
N_KEYS = 128
PEER_TOPK = 16
PEER_HK = PEER_HEADS * PEER_TOPK
ADA_CHUNKS = 6
EPS = 1e-6

LANES = 128
SUBLANES = 8
SC_LANES = 16
VMEM_LIMIT = 48 * 1024 * 1024

F32 = jnp.float32
BF16 = jnp.bfloat16
NEG_INF = float("-inf")


def _cparams(*sem):
    return pltpu.CompilerParams(dimension_semantics=sem, vmem_limit_bytes=VMEM_LIMIT)


def _norm_mod(x, gn, sc, sh):
    ms = jnp.mean(x * x, axis=-1, keepdims=True)
    return (x * lax.rsqrt(ms + EPS)) * gn * (1.0 + sc) + sh


def _ada_kernel(c_ref, w_ref, b_ref, o_ref):
    c = c_ref[...]
    ca = c * jax.nn.sigmoid(c)
    o_ref[0] = jnp.dot(ca.astype(BF16), w_ref[0].astype(BF16),
                       preferred_element_type=F32) + b_ref[0]


def _ada_mod(c, ada_w, ada_b):
    depth, d, n = ada_w.shape
    bsz = c.shape[0]
    tn = 512
    return pl.pallas_call(
        _ada_kernel,
        grid=(depth, n // tn),
        in_specs=[pl.BlockSpec((bsz, d), lambda i, j: (0, 0)),
                  pl.BlockSpec((1, d, tn), lambda i, j: (i, 0, j)),
                  pl.BlockSpec((1, 1, tn), lambda i, j: (i, 0, j))],
        out_specs=pl.BlockSpec((1, bsz, tn), lambda i, j: (i, 0, j)),
        out_shape=jax.ShapeDtypeStruct((depth, bsz, n), F32),
        compiler_params=_cparams("parallel", "parallel"),
        name="ada_mod",
    )(c, ada_w, ada_b.reshape(depth, 1, n))


def _norm_mm_kernel(x_ref, gn_ref, sc_ref, sh_ref, w_ref, o_ref):
    h = _norm_mod(x_ref[...], gn_ref[...], sc_ref[0], sh_ref[0])
    o_ref[...] = jnp.dot(h.astype(BF16), w_ref[...],
                         preferred_element_type=F32).astype(o_ref.dtype)


def _norm_mm(x2, seq, gn, sc, sh, w_bf16, out_dtype, tm=256):
    t, d = x2.shape
    n = w_bf16.shape[1]
    per_seq = lambda i: ((i * tm) // seq, 0, 0)
    return pl.pallas_call(
        _norm_mm_kernel,
        grid=(t // tm,),
        in_specs=[pl.BlockSpec((tm, d), lambda i: (i, 0)),
                  pl.BlockSpec((1, d), lambda i: (0, 0)),
                  pl.BlockSpec((1, 1, d), per_seq),
                  pl.BlockSpec((1, 1, d), per_seq),
                  pl.BlockSpec((d, n), lambda i: (0, 0))],
        out_specs=pl.BlockSpec((tm, n), lambda i: (i, 0)),
        out_shape=jax.ShapeDtypeStruct((t, n), out_dtype),
        compiler_params=_cparams("parallel"),
        name="norm_mm",
    )(x2, gn.reshape(1, d), sc, sh, w_bf16)


def _norm_glu_kernel(x_ref, gn_ref, sc_ref, sh_ref, wa_ref, wg_ref, ba_ref, bg_ref, o_ref):
    h = _norm_mod(x_ref[...], gn_ref[...], sc_ref[0], sh_ref[0]).astype(BF16)
    a = jnp.dot(h, wa_ref[...], preferred_element_type=F32) + ba_ref[...]
    g = jnp.dot(h, wg_ref[...], preferred_element_type=F32) + bg_ref[...]
    o_ref[...] = a * jax.nn.sigmoid(g)


def _norm_glu(x2, seq, gn, sc, sh, w_in_bf16, b_in, seq0, nseq, tm=256):
    d = x2.shape[1]
    t = nseq * seq
    n = w_in_bf16.shape[1] // 2
    blk0 = seq0 * seq // tm
    per_seq = lambda i: ((i * tm) // seq + seq0, 0, 0)
    b2 = b_in.reshape(1, 2 * n)
    return pl.pallas_call(
        _norm_glu_kernel,
        grid=(t // tm,),
        in_specs=[pl.BlockSpec((tm, d), lambda i: (i + blk0, 0)),
                  pl.BlockSpec((1, d), lambda i: (0, 0)),
                  pl.BlockSpec((1, 1, d), per_seq),
                  pl.BlockSpec((1, 1, d), per_seq),
                  pl.BlockSpec((d, n), lambda i: (0, 0)),
                  pl.BlockSpec((d, n), lambda i: (0, 1)),
                  pl.BlockSpec((1, n), lambda i: (0, 0)),
                  pl.BlockSpec((1, n), lambda i: (0, 1))],
        out_specs=pl.BlockSpec((tm, n), lambda i: (i, 0)),
        out_shape=jax.ShapeDtypeStruct((t, n), F32),
        compiler_params=_cparams("parallel"),
        name="norm_glu",
    )(x2, gn.reshape(1, d), sc, sh, w_in_bf16, w_in_bf16, b2, b2)


HALO = 32


def _conv_kernel(cur_ref, prev_ref, wdw_ref, bdw_ref, lng_ref, lnb_ref, wout_ref,
                 bout_ref, x_ref, g_ref, o_ref, win_s):
    ts = cur_ref.shape[1]
    first = pl.program_id(1) == 0
    win_s[0:HALO, :] = jnp.where(first, 0.0, prev_ref[0])
    win_s[HALO:HALO + ts, :] = cur_ref[0]
    acc = jnp.zeros(cur_ref.shape[1:], F32) + bdw_ref[...]
    off = HALO - (CONV_WIDTH - 1)
    for k in range(CONV_WIDTH):
        acc = acc + win_s[off + k:off + k + ts, :] * wdw_ref[k:k + 1, :]
    mu = jnp.mean(acc, axis=-1, keepdims=True)
    cen = acc - mu
    var = jnp.mean(cen * cen, axis=-1, keepdims=True)
    y = cen * lax.rsqrt(var + EPS) * lng_ref[...] + lnb_ref[...]
    y = y * jax.nn.sigmoid(y)
    m = jnp.dot(y.astype(BF16), wout_ref[...], preferred_element_type=F32) + bout_ref[...]
    o_ref[0] = x_ref[0] + g_ref[0] * m


def _conv_block(u3, x3, g1, w_dw, b_dw, ln_g, ln_b, w_out_bf16, b_out, seq0, ts=256):
    bsz, seq, d = u3.shape
    r = ts // HALO
    wdw = jnp.concatenate([w_dw, jnp.zeros((HALO - CONV_WIDTH, d), F32)], axis=0)
    vec = lambda b, i: (0, 0)
    return pl.pallas_call(
        _conv_kernel,
        grid=(bsz, seq // ts),
        in_specs=[pl.BlockSpec((1, ts, d), lambda b, i: (b, i, 0)),
                  pl.BlockSpec((1, HALO, d), lambda b, i: (b, jnp.maximum(i * r - 1, 0), 0)),
                  pl.BlockSpec((HALO, d), vec),
                  pl.BlockSpec((1, d), vec),
                  pl.BlockSpec((1, d), vec),
                  pl.BlockSpec((1, d), vec),
                  pl.BlockSpec((d, d), vec),
                  pl.BlockSpec((1, d), vec),
                  pl.BlockSpec((1, ts, d), lambda b, i: (b + seq0, i, 0)),
                  pl.BlockSpec((1, 1, d), lambda b, i: (b + seq0, 0, 0))],
        out_specs=pl.BlockSpec((1, ts, d), lambda b, i: (b, i, 0)),
        out_shape=jax.ShapeDtypeStruct((bsz, seq, d), F32),
        scratch_shapes=[pltpu.VMEM((HALO + ts, d), F32)],
        compiler_params=_cparams("parallel", "arbitrary"),
        name="conv_block",
    )(u3, u3, wdw, b_dw.reshape(1, d), ln_g.reshape(1, d), ln_b.reshape(1, d),
      w_out_bf16, b_out.reshape(1, d), x3, g1)


ATT_T = 256


def _attn_kernel(q_ref, k_ref, v_ref, o_ref):
    t = ATT_T
    qi = pl.program_id(2)
    q2 = q_ref[0] * (1.0 / math.sqrt(HEAD_DIM))
    lane = lax.broadcasted_iota(jnp.int32, (1, LANES), 1)
    head_mask = (lane < HEAD_DIM, lane >= HEAD_DIM)
    row = lax.broadcasted_iota(jnp.int32, (t, t), 0)
    col = lax.broadcasted_iota(jnp.int32, (t, t), 1)
    tri = (row >= col).astype(BF16)
    tri2 = jnp.concatenate([tri, tri], axis=0)
    qh = tuple(jnp.where(m, q2, jnp.zeros_like(q2)) for m in head_mask)
    nt = (((1,), (1,)), ((), ()))

    def tile(kt, suf, acc, diag):
        start = pl.multiple_of(kt * t, t)
        k2 = k_ref[0, pl.ds(start, t), :]
        v2 = v_ref[0, pl.ds(start, t), :]
        new_suf = []
        for h in range(2):
            z = lax.dot_general(qh[h], k2, nt, preferred_element_type=F32)
            sp = jnp.maximum(z, 0.0) + jnp.log(1.0 + jnp.exp(-jnp.abs(z)))
            if diag:
                sp = jnp.where(col < row, sp, 0.0)
            hi = sp.astype(BF16)
            mid = (sp - hi.astype(F32)).astype(BF16)
            cs = jnp.dot(jnp.concatenate([hi, mid], axis=1), tri2,
                         preferred_element_type=F32)
            a = jnp.exp(z - cs - suf[h])
            if diag:
                a = jnp.where(col < row, a, 0.0)
            pv = jnp.dot(a.astype(BF16), v2, preferred_element_type=F32)
            acc = acc + jnp.where(head_mask[h], pv, 0.0)
            new_suf.append(suf[h] + cs[:, 0:1])
        return tuple(new_suf), acc

    zero = jnp.zeros((t, 1), F32)
    suf, acc = tile(qi, (zero, zero), jnp.zeros((t, LANES), F32), True)

    def body(n, carry):
        s0, s1, acc = carry
        (s0, s1), acc = tile(qi - n, (s0, s1), acc, False)
        return s0, s1, acc

    _, _, acc = lax.fori_loop(1, qi + 1, body, (suf[0], suf[1], acc))
    o_ref[0] = acc


def _attention(qkv3):
    bsz, seq, n3 = qkv3.shape
    d = n3 // 3
    nb = d // LANES
    return pl.pallas_call(
        _attn_kernel,
        grid=(bsz, nb, seq // ATT_T),
        in_specs=[pl.BlockSpec((1, ATT_T, LANES), lambda b, h, i: (b, i, h)),
                  pl.BlockSpec((1, seq, LANES), lambda b, h, i: (b, 0, nb + h)),
                  pl.BlockSpec((1, seq, LANES), lambda b, h, i: (b, 0, 2 * nb + h))],
        out_specs=pl.BlockSpec((1, ATT_T, LANES), lambda b, h, i: (b, i, h)),
        out_shape=jax.ShapeDtypeStruct((bsz, seq, d), F32),
        compiler_params=_cparams("parallel", "parallel", "arbitrary"),
        name="sb_attention",
    )(qkv3, qkv3, qkv3)


def _mm_res_kernel(a_ref, w_ref, x_ref, g_ref, o_ref):
    m = jnp.dot(a_ref[...].astype(BF16), w_ref[...], preferred_element_type=F32)
    o_ref[...] = x_ref[...] + g_ref[0] * m


def _mm_res(a2, w_bf16, x2, g, seq, tm=512):
    t, d = x2.shape
    k = a2.shape[1]
    return pl.pallas_call(
        _mm_res_kernel,
        grid=(t // tm,),
        in_specs=[pl.BlockSpec((tm, k), lambda i: (i, 0)),
                  pl.BlockSpec((k, d), lambda i: (0, 0)),
                  pl.BlockSpec((tm, d), lambda i: (i, 0)),
                  pl.BlockSpec((1, 1, d), lambda i: ((i * tm) // seq, 0, 0))],
        out_specs=pl.BlockSpec((tm, d), lambda i: (i, 0)),
        out_shape=jax.ShapeDtypeStruct((t, d), F32),
        compiler_params=_cparams("parallel"),
        name="mm_res",
    )(a2, w_bf16, x2, g)


PAIR_ROWS = tuple(PEER_TOPK // (i + 1) for i in range(PEER_TOPK))


def _extract_top(s, payload, k):
    rows = lax.broadcasted_iota(jnp.int32, s.shape, 0)
    big = s.shape[0]
    vals, outs = [], []
    for _ in range(k):
        m = jnp.max(s, axis=0, keepdims=True)
        r = jnp.min(jnp.where(s == m, rows, big), axis=0, keepdims=True)
        sel = rows == r
        vals.append(m)
        if payload is None:
            outs.append(r)
        else:
            outs.append(jnp.max(jnp.where(sel, payload, -1), axis=0, keepdims=True))
        s = jnp.where(sel, NEG_INF, s)
    return jnp.concatenate(vals, axis=0), jnp.concatenate(outs, axis=0)


def _pair_candidates(a, ia, b, ib):
    sub = lax.broadcasted_iota(jnp.int32, (SUBLANES, a.shape[1]), 0)
    cand, ids = [], []
    for i in range(SUBLANES):
        for j0 in range(0, PAIR_ROWS[i], SUBLANES):
            n = min(PAIR_ROWS[i] - j0, SUBLANES)
            c = a[i:i + 1, :] + b[j0:j0 + SUBLANES, :]
            e = ia[i:i + 1, :] * N_KEYS + ib[j0:j0 + SUBLANES, :]
            if n < SUBLANES:
                c = jnp.where(sub < n, c, NEG_INF)
            cand.append(c)
            ids.append(e)
    cand.append(a[SUBLANES:, :] + b[0:1, :])
    ids.append(ia[SUBLANES:, :] * N_KEYS + ib[0:1, :])
    return jnp.concatenate(cand, axis=0), jnp.concatenate(ids, axis=0)


def _peer_route_kernel(row_base, x_ref, gn_ref, sc_ref, sh_ref, wq_ref, keys_ref,
                       h_ref, e_ref, g_ref, q_s, e_s, g_s):
    h = _norm_mod(x_ref[...], gn_ref[...], sc_ref[0], sh_ref[0])
    h_ref[...] = h
    q = jnp.dot(h.astype(BF16), wq_ref[...], preferred_element_type=F32)
    for j in range(2 * PEER_HEADS):
        q_s[j] = q[:, j * PEER_DQH:(j + 1) * PEER_DQH].astype(BF16)
    nt = (((1,), (1,)), ((), ()))

    def head(hd, carry):
        tops = []
        for c in range(2):
            s_t = lax.dot_general(keys_ref[2 * hd + c], q_s[2 * hd + c], nt,
                                  preferred_element_type=F32)
            tops.append(_extract_top(s_t, None, PEER_TOPK))
        cand, ids = _pair_candidates(tops[0][0], tops[0][1], tops[1][0], tops[1][1])
        best, experts = _extract_top(cand, ids, PEER_TOPK)
        ex = jnp.exp(best - best[0:1, :])
        g_s[hd] = ex / jnp.sum(ex, axis=0, keepdims=True)
        e_s[hd] = experts
        return carry

    lax.fori_loop(0, PEER_HEADS, head, 0)
    tm = x_ref.shape[0]
    e_ref[...] = e_s[...].reshape(PEER_HK, tm).T + row_base
    g_ref[...] = g_s[...].reshape(PEER_HK, tm).T


def _peer_route(x2, seq, gn, sc, sh, wq_bf16, keys_bf16, row_base, tm=256):
    t, d = x2.shape
    nq = wq_bf16.shape[1]
    per_seq = lambda i: ((i * tm) // seq, 0, 0)
    return pl.pallas_call(
        functools.partial(_peer_route_kernel, row_base),
        grid=(t // tm,),
        in_specs=[pl.BlockSpec((tm, d), lambda i: (i, 0)),
                  pl.BlockSpec((1, d), lambda i: (0, 0)),
                  pl.BlockSpec((1, 1, d), per_seq),
                  pl.BlockSpec((1, 1, d), per_seq),
                  pl.BlockSpec((d, nq), lambda i: (0, 0)),
                  pl.BlockSpec((2 * PEER_HEADS, N_KEYS, PEER_DQH), lambda i: (0, 0, 0))],
        out_specs=[pl.BlockSpec((tm, d), lambda i: (i, 0)),
                   pl.BlockSpec((tm, PEER_HK), lambda i: (i, 0)),
                   pl.BlockSpec((tm, PEER_HK), lambda i: (i, 0))],
        out_shape=[jax.ShapeDtypeStruct((t, d), F32),
                   jax.ShapeDtypeStruct((t, PEER_HK), jnp.int32),
                   jax.ShapeDtypeStruct((t, PEER_HK), F32)],
        scratch_shapes=[pltpu.VMEM((2 * PEER_HEADS, tm, PEER_DQH), BF16),
                        pltpu.VMEM((PEER_HEADS, PEER_TOPK, tm), jnp.int32),
                        pltpu.VMEM((PEER_HEADS, PEER_TOPK, tm), F32)],
        compiler_params=_cparams("parallel"),
        name="peer_route",
    )(x2, gn.reshape(1, d), sc, sh, wq_bf16, keys_bf16)


SC_ROWS = 16
SC_GROUPS = PEER_HK // SC_ROWS
SC_TB = 32
SC_CHUNKS = D_MODEL // SC_LANES
SC_HALF = D_MODEL // 2
SC_WCH = SC_HALF // SC_LANES
SC_NBUF = 4


def _pack_table(tab):
    b = lax.bitcast_convert_type(tab.reshape(-1, D_MODEL).astype(BF16), jnp.uint16)
    b = b.astype(jnp.uint32)
    return lax.bitcast_convert_type(b[:, :SC_HALF] | (b[:, SC_HALF:] << 16), jnp.int32)


def _unpack_pair(w):
    lo = plsc.bitcast(lax.shift_left(w, jnp.int32(16)), F32)
    hi = plsc.bitcast(w & jnp.int32(-65536), F32)
    return lo, hi


SC_BF16_GROUP = 4


def _group_products(ws, xs):
    prods = [plsc.bitcast(w, BF16) * x for w, x in zip(ws, xs)]
    return _unpack_pair(plsc.bitcast(_tree_sum(prods), jnp.int32))


def _sc_mesh():
    return plsc.VectorSubcoreMesh(core_axis_name="c", subcore_axis_name="s")


def _sc_pipeline(n_items, gather, compute):
    ahead = SC_NBUF - 1
    for i in range(ahead):
        gather(i, i).start()

    @pl.loop(0, n_items, step=SC_NBUF)
    def _(it):
        for b in range(SC_NBUF):
            i = it + b
            gather(i, b).wait()

            @pl.when(i + ahead < n_items)
            def _():
                gather(i + ahead, (b + ahead) % SC_NBUF).start()

            compute(i, b)


def _tree_sum(terms):
    while len(terms) > 1:
        terms = [terms[i] + terms[i + 1] for i in range(0, len(terms), 2)]
    return terms[0]


def _peer_pre(h_pk, idx, u_tab):
    t = h_pk.shape[0]
    info = plsc.get_sparse_core_info()
    nc, nw = info.num_cores, info.num_cores * info.num_subcores
    tpw = t // nw
    n_items = SC_TB * SC_GROUPS

    @functools.partial(
        pl.kernel, mesh=_sc_mesh(),
        out_type=jax.ShapeDtypeStruct((t, PEER_HK), F32),
        scratch_types=[pltpu.VMEM((SC_TB, PEER_HK), jnp.int32),
                       pltpu.VMEM((SC_TB, SC_HALF), jnp.int32),
                       pltpu.VMEM((SC_TB, PEER_HK), F32),
                       pltpu.VMEM((SC_ROWS, SC_LANES), F32)]
                      + [pltpu.VMEM((SC_ROWS, SC_HALF), jnp.int32)] * SC_NBUF
                      + [pltpu.SemaphoreType.DMA] * SC_NBUF,
        compiler_params=pltpu.CompilerParams(needs_layout_passes=False),
        name="peer_pre")
    def k(h_hbm, idx_hbm, u_hbm, p_hbm, idx_v, h_v, p_v, tr_v, *ring):
        wid = lax.axis_index("s") * nc + lax.axis_index("c")
        bufs, sems = ring[:SC_NBUF], ring[SC_NBUF:]
        lane = lax.iota(jnp.int32, SC_LANES)

        def gather(it, b):
            tl, g = it // SC_GROUPS, it % SC_GROUPS
            return pltpu.make_async_copy(
                u_hbm.at[idx_v.at[tl, pl.ds(g * SC_ROWS, SC_ROWS)]], bufs[b], sems[b])

        def compute(it, b):
            tl, g = it // SC_GROUPS, it % SC_GROUPS
            buf = bufs[b]

            def body(c, accs):
                sls = [pl.ds((c * SC_BF16_GROUP + q) * SC_LANES, SC_LANES)
                       for q in range(SC_BF16_GROUP)]
                xs = [plsc.bitcast(h_v[tl, sl], BF16) for sl in sls]
                out = []
                for r in range(SC_ROWS):
                    lo, hi = _group_products([buf[r, sl] for sl in sls], xs)
                    out.append(accs[r] + (lo + hi))
                return tuple(out)

            zero = jnp.zeros((SC_LANES,), F32)
            accs = lax.fori_loop(0, SC_WCH // SC_BF16_GROUP, body, (zero,) * SC_ROWS)
            skew = [(lane + j) & (SC_LANES - 1) for j in range(SC_LANES)]
            for r in range(SC_ROWS):
                plsc.store_scatter(tr_v, [jnp.full((SC_LANES,), r, jnp.int32), skew[r]], accs[r])
            cols = [plsc.load_gather(tr_v, [lane, skew[j]]) for j in range(SC_LANES)]
            p_v[tl, pl.ds(g * SC_ROWS, SC_ROWS)] = _tree_sum(cols)

        @pl.loop(0, tpw // SC_TB)
        def _(blk):
            t0 = wid * tpw + blk * SC_TB
            pltpu.sync_copy(idx_hbm.at[pl.ds(t0, SC_TB)], idx_v)
            pltpu.sync_copy(h_hbm.at[pl.ds(t0, SC_TB)], h_v)
            _sc_pipeline(n_items, gather, compute)
            pltpu.sync_copy(p_v, p_hbm.at[pl.ds(t0, SC_TB)])

    return k(h_pk, idx, u_tab)


def _peer_post(act, idx, v_tab):
    t = act.shape[0]
    d = 2 * v_tab.shape[1]
    info = plsc.get_sparse_core_info()
    nc, nw = info.num_cores, info.num_cores * info.num_subcores
    tpw = t // nw
    n_items = SC_TB * SC_GROUPS

    @functools.partial(
        pl.kernel, mesh=_sc_mesh(),
        out_type=jax.ShapeDtypeStruct((t, d), F32),
        scratch_types=[pltpu.VMEM((SC_TB, PEER_HK), jnp.int32),
                       pltpu.VMEM((SC_TB, PEER_HK), jnp.int32),
                       pltpu.VMEM((SC_TB, d), F32)]
                      + [pltpu.VMEM((SC_ROWS, SC_HALF), jnp.int32)] * SC_NBUF
                      + [pltpu.SemaphoreType.DMA] * SC_NBUF,
        compiler_params=pltpu.CompilerParams(needs_layout_passes=False),
        name="peer_post")
    def k(a_hbm, idx_hbm, v_hbm, y_hbm, idx_v, a_v, y_v, *ring):
        wid = lax.axis_index("s") * nc + lax.axis_index("c")
        bufs, sems = ring[:SC_NBUF], ring[SC_NBUF:]

        def gather(it, b):
            tl, g = it // SC_GROUPS, it % SC_GROUPS
            return pltpu.make_async_copy(
                v_hbm.at[idx_v.at[tl, pl.ds(g * SC_ROWS, SC_ROWS)]], bufs[b], sems[b])

        def compute(it, b):
            tl, g = it // SC_GROUPS, it % SC_GROUPS
            buf = bufs[b]
            tlv = jnp.full((SC_LANES,), tl, jnp.int32)
            w = tuple(plsc.bitcast(plsc.load_gather(
                a_v, [tlv, jnp.full((SC_LANES,), g * SC_ROWS + r, jnp.int32)]), BF16)
                for r in range(SC_ROWS))

            @plsc.parallel_loop(0, SC_WCH, unroll=2)
            def _(c):
                sl = pl.ds(c * SC_LANES, SC_LANES)
                parts = [_group_products([buf[r, sl] for r in range(g0, g0 + SC_BF16_GROUP)],
                                         w[g0:g0 + SC_BF16_GROUP])
                         for g0 in range(0, SC_ROWS, SC_BF16_GROUP)]
                plsc.addupdate(y_v.at[tl, sl], _tree_sum([p[0] for p in parts]))
                plsc.addupdate(y_v.at[tl, pl.ds(SC_HALF + c * SC_LANES, SC_LANES)],
                               _tree_sum([p[1] for p in parts]))

        @pl.loop(0, tpw // SC_TB)
        def _(blk):
            t0 = wid * tpw + blk * SC_TB
            pltpu.sync_copy(idx_hbm.at[pl.ds(t0, SC_TB)], idx_v)
            pltpu.sync_copy(a_hbm.at[pl.ds(t0, SC_TB)], a_v)

            @pl.loop(0, SC_TB)
            def _(r):
                @plsc.parallel_loop(0, SC_CHUNKS, unroll=4)
                def _(c):
                    y_v[r, pl.ds(c * SC_LANES, SC_LANES)] = jnp.zeros((SC_LANES,), F32)

            _sc_pipeline(n_items, gather, compute)
            pltpu.sync_copy(y_v, y_hbm.at[pl.ds(t0, SC_TB)])

    return k(act, idx, v_tab)


def _act_kernel(p_ref, g_ref, o_ref):
    p = p_ref[...]
    o_ref[...] = (p * (lax.erf(p * (1.0 / math.sqrt(2.0))) + 1.0) * 0.5) * g_ref[...]


def _expert_act(p, gates, tm=512):
    t, n = p.shape
    spec = pl.BlockSpec((tm, n), lambda i: (i, 0))
    return pl.pallas_call(
        _act_kernel, grid=(t // tm,), in_specs=[spec, spec], out_specs=spec,
        out_shape=jax.ShapeDtypeStruct((t, n), F32),
        compiler_params=_cparams("parallel"), name="expert_act",
    )(p, gates)


def _res_kernel(x_ref, y_ref, g_ref, o_ref):
    o_ref[...] = x_ref[...] + g_ref[0] * y_ref[...]


def _res_norm_kernel(x_ref, y_ref, g_ref, fg_ref, o_ref):
    x = x_ref[...] + g_ref[0] * y_ref[...]
    ms = jnp.mean(x * x, axis=-1, keepdims=True)
    o_ref[...] = x * lax.rsqrt(ms + EPS) * fg_ref[...]


def _residual(x2, y2, g, seq, final_g=None, tm=512):
    t, d = x2.shape
    tile = pl.BlockSpec((tm, d), lambda i: (i, 0))
    gspec = pl.BlockSpec((1, 1, d), lambda i: ((i * tm) // seq, 0, 0))
    if final_g is None:
        kern, ins, args = _res_kernel, [tile, tile, gspec], (x2, y2, g)
    else:
        kern = _res_norm_kernel
        ins = [tile, tile, gspec, pl.BlockSpec((1, d), lambda i: (0, 0))]
        args = (x2, y2, g, final_g.reshape(1, d))
    return pl.pallas_call(
        kern, grid=(t // tm,), in_specs=ins, out_specs=tile,
        out_shape=jax.ShapeDtypeStruct((t, d), F32),
        compiler_params=_cparams("parallel"), name="residual",
    )(*args)


def _pack_same(a):
    b = lax.bitcast_convert_type(a.astype(BF16), jnp.uint16).astype(jnp.uint32)
    return lax.bitcast_convert_type(b | (b << 16), jnp.int32)


def _peer(x2, seq, gn, sc, sh, wq_bf16, keys_bf16, u_tab, v_tab):
    h2, experts, gates = _peer_route(x2, seq, gn, sc, sh, wq_bf16, keys_bf16, 0)
    p = _peer_pre(_pack_table(h2), experts, u_tab)
    act = _expert_act(p, gates)
    return _peer_post(_pack_same(act), experts, v_tab)


BATCH_SPLIT = 16


def kernel(x, c, ada_w, ada_b, norm_mix_g, norm_ffn_g, cv_w_in, cv_b_in, cv_w_dw, cv_b_dw, cv_ln_g, cv_ln_b, cv_w_out, cv_b_out, sb_w_qkv, sb_w_o, pk_w_q, pk_keys, pk_u, pk_v, final_g):
    bsz, seq, d = x.shape
    mod = _ada_mod(c, ada_w, ada_b)
    w_in, w_out = cv_w_in.astype(BF16), cv_w_out.astype(BF16)
    w_qkv, w_o = sb_w_qkv.astype(BF16), sb_w_o.astype(BF16)
    w_q = pk_w_q.astype(BF16)
    keys = pk_keys.reshape(DEPTH, 2 * PEER_HEADS, N_KEYS, PEER_DQH).astype(BF16)
    u_tabs = [_pack_table(pk_u[i]) for i in range(DEPTH)]
    v_tabs = [_pack_table(pk_v[i]) for i in range(DEPTH)]
    nb = bsz // BATCH_SPLIT
    t = nb * seq
    outs = []
    for part in range(BATCH_SPLIT):
        lo = part * nb
        x2 = None
        for i in range(DEPTH):
            full = [mod[i, :, n * d:(n + 1) * d].reshape(bsz, 1, d) for n in range(ADA_CHUNKS)]
            sh1, sc1, g1, sh2, sc2, g2 = (m[lo:lo + nb] for m in full)
            j = i // 2
            if i % 2 == 0:
                if i == 0:
                    xin, scv, shv, gv, seq0 = x, full[1], full[0], full[2], lo
                else:
                    xin, scv, shv, gv, seq0 = x2.reshape(nb, seq, d), sc1, sh1, g1, 0
                u2 = _norm_glu(xin.reshape(-1, d), seq, norm_mix_g[i], scv, shv,
                               w_in[j], cv_b_in[j], seq0, nb)
                x2 = _conv_block(u2.reshape(nb, seq, d), xin, gv,
                                 cv_w_dw[j], cv_b_dw[j], cv_ln_g[j], cv_ln_b[j],
                                 w_out[j], cv_b_out[j], seq0).reshape(t, d)
            else:
                qkv = _norm_mm(x2, seq, norm_mix_g[i], sc1, sh1, w_qkv[j], BF16)
                o = _attention(qkv.reshape(nb, seq, 3 * d))
                x2 = _mm_res(o.reshape(t, d), w_o[j], x2, g1, seq)
            y2 = _peer(x2, seq, norm_ffn_g[i], sc2, sh2, w_q[i], keys[i], u_tabs[i], v_tabs[i])
            x2 = _residual(x2, y2, g2, seq, final_g if i == DEPTH - 1 else None)
        outs.append(x2.reshape(nb, seq, d))
    return jnp.concatenate(outs, axis=0)
```

```python
import functools
import math

import jax
import jax.numpy as jnp
from jax import lax
from jax.experimental import pallas as pl
from jax.experimental.pallas import tpu as pltpu
from jax.experimental.pallas import tpu_sc as plsc

D_MODEL = 1024
DEPTH = 2
CONV_WIDTH = 31
N_HEADS = 16
HEAD_DIM = 64
PEER_HEADS = 8
PEER_DQH = 128
N_KEYS = 128
PEER_TOPK = 16
PEER_HK = PEER_HEADS * PEER_TOPK
ADA_CHUNKS = 6
EPS = 1e-6

LANES = 128
SUBLANES = 8
SC_LANES = 16
VMEM_LIMIT = 48 * 1024 * 1024

F32 = jnp.float32
BF16 = jnp.bfloat16
NEG_INF = float("-inf")


def _cparams(*sem):
    return pltpu.CompilerParams(dimension_semantics=sem, vmem_limit_bytes=VMEM_LIMIT)


def _norm_mod(x, gn, sc, sh):
    ms = jnp.mean(x * x, axis=-1, keepdims=True)
    return (x * lax.rsqrt(ms + EPS)) * gn * (1.0 + sc) + sh


def _ada_kernel(c_ref, w_ref, b_ref, o_ref):
    c = c_ref[...]
    ca = c * jax.nn.sigmoid(c)
    o_ref[0] = jnp.dot(ca.astype(BF16), w_ref[0].astype(BF16),
                       preferred_element_type=F32) + b_ref[0]


def _ada_mod(c, ada_w, ada_b):
    depth, d, n = ada_w.shape
    bsz = c.shape[0]
    tn = 512
    return pl.pallas_call(
        _ada_kernel,
        grid=(depth, n // tn),
        in_specs=[pl.BlockSpec((bsz, d), lambda i, j: (0, 0)),
                  pl.BlockSpec((1, d, tn), lambda i, j: (i, 0, j)),
                  pl.BlockSpec((1, 1, tn), lambda i, j: (i, 0, j))],
        out_specs=pl.BlockSpec((1, bsz, tn), lambda i, j: (i, 0, j)),
        out_shape=jax.ShapeDtypeStruct((depth, bsz, n), F32),
        compiler_params=_cparams("parallel", "parallel"),
        name="ada_mod",
    )(c, ada_w, ada_b.reshape(depth, 1, n))


def _norm_mm_kernel(x_ref, gn_ref, sc_ref, sh_ref, w_ref, o_ref):
    h = _norm_mod(x_ref[...], gn_ref[...], sc_ref[0], sh_ref[0])
    o_ref[...] = jnp.dot(h.astype(BF16), w_ref[...],
                         preferred_element_type=F32).astype(o_ref.dtype)


def _norm_mm(x2, seq, gn, sc, sh, w_bf16, out_dtype, tm=256):
    t, d = x2.shape
    n = w_bf16.shape[1]
    per_seq = lambda i: ((i * tm) // seq, 0, 0)
    return pl.pallas_call(
        _norm_mm_kernel,
        grid=(t // tm,),
        in_specs=[pl.BlockSpec((tm, d), lambda i: (i, 0)),
                  pl.BlockSpec((1, d), lambda i: (0, 0)),
                  pl.BlockSpec((1, 1, d), per_seq),
                  pl.BlockSpec((1, 1, d), per_seq),
                  pl.BlockSpec((d, n), lambda i: (0, 0))],
        out_specs=pl.BlockSpec((tm, n), lambda i: (i, 0)),
        out_shape=jax.ShapeDtypeStruct((t, n), out_dtype),
        compiler_params=_cparams("parallel"),
        name="norm_mm",
    )(x2, gn.reshape(1, d), sc, sh, w_bf16)


def _norm_glu_kernel(x_ref, gn_ref, sc_ref, sh_ref, wa_ref, wg_ref, ba_ref, bg_ref, o_ref):
    h = _norm_mod(x_ref[...], gn_ref[...], sc_ref[0], sh_ref[0]).astype(BF16)
    a = jnp.dot(h, wa_ref[...], preferred_element_type=F32) + ba_ref[...]
    g = jnp.dot(h, wg_ref[...], preferred_element_type=F32) + bg_ref[...]
    o_ref[...] = a * jax.nn.sigmoid(g)


def _norm_glu(x2, seq, gn, sc, sh, w_in_bf16, b_in, seq0, nseq, tm=256):
    d = x2.shape[1]
    t = nseq * seq
    n = w_in_bf16.shape[1] // 2
    blk0 = seq0 * seq // tm
    per_seq = lambda i: ((i * tm) // seq + seq0, 0, 0)
    b2 = b_in.reshape(1, 2 * n)
    return pl.pallas_call(
        _norm_glu_kernel,
        grid=(t // tm,),
        in_specs=[pl.BlockSpec((tm, d), lambda i: (i + blk0, 0)),
                  pl.BlockSpec((1, d), lambda i: (0, 0)),
                  pl.BlockSpec((1, 1, d), per_seq),
                  pl.BlockSpec((1, 1, d), per_seq),
                  pl.BlockSpec((d, n), lambda i: (0, 0)),
                  pl.BlockSpec((d, n), lambda i: (0, 1)),
                  pl.BlockSpec((1, n), lambda i: (0, 0)),
                  pl.BlockSpec((1, n), lambda i: (0, 1))],
        out_specs=pl.BlockSpec((tm, n), lambda i: (i, 0)),
        out_shape=jax.ShapeDtypeStruct((t, n), F32),
        compiler_params=_cparams("parallel"),
        name="norm_glu",
    )(x2, gn.reshape(1, d), sc, sh, w_in_bf16, w_in_bf16, b2, b2)


HALO = 32


def _conv_kernel(cur_ref, prev_ref, wdw_ref, bdw_ref, lng_ref, lnb_ref, wout_ref,
                 bout_ref, x_ref, g_ref, o_ref, win_s):
    ts = cur_ref.shape[1]
    first = pl.program_id(1) == 0
    win_s[0:HALO, :] = jnp.where(first, 0.0, prev_ref[0])
    win_s[HALO:HALO + ts, :] = cur_ref[0]
    acc = jnp.zeros(cur_ref.shape[1:], F32) + bdw_ref[...]
    off = HALO - (CONV_WIDTH - 1)
    for k in range(CONV_WIDTH):
        acc = acc + win_s[off + k:off + k + ts, :] * wdw_ref[k:k + 1, :]
    mu = jnp.mean(acc, axis=-1, keepdims=True)
    cen = acc - mu
    var = jnp.mean(cen * cen, axis=-1, keepdims=True)
    y = cen * lax.rsqrt(var + EPS) * lng_ref[...] + lnb_ref[...]
    y = y * jax.nn.sigmoid(y)
    m = jnp.dot(y.astype(BF16), wout_ref[...], preferred_element_type=F32) + bout_ref[...]
    o_ref[0] = x_ref[0] + g_ref[0] * m


def _conv_block(u3, x3, g1, w_dw, b_dw, ln_g, ln_b, w_out_bf16, b_out, seq0, ts=256):
    bsz, seq, d = u3.shape
    r = ts // HALO
    wdw = jnp.concatenate([w_dw, jnp.zeros((HALO - CONV_WIDTH, d), F32)], axis=0)
    vec = lambda b, i: (0, 0)
    return pl.pallas_call(
        _conv_kernel,
        grid=(bsz, seq // ts),
        in_specs=[pl.BlockSpec((1, ts, d), lambda b, i: (b, i, 0)),
                  pl.BlockSpec((1, HALO, d), lambda b, i: (b, jnp.maximum(i * r - 1, 0), 0)),
                  pl.BlockSpec((HALO, d), vec),
                  pl.BlockSpec((1, d), vec),
                  pl.BlockSpec((1, d), vec),
                  pl.BlockSpec((1, d), vec),
                  pl.BlockSpec((d, d), vec),
                  pl.BlockSpec((1, d), vec),
                  pl.BlockSpec((1, ts, d), lambda b, i: (b + seq0, i, 0)),
                  pl.BlockSpec((1, 1, d), lambda b, i: (b + seq0, 0, 0))],
        out_specs=pl.BlockSpec((1, ts, d), lambda b, i: (b, i, 0)),
        out_shape=jax.ShapeDtypeStruct((bsz, seq, d), F32),
        scratch_shapes=[pltpu.VMEM((HALO + ts, d), F32)],
        compiler_params=_cparams("parallel", "arbitrary"),
        name="conv_block",
    )(u3, u3, wdw, b_dw.reshape(1, d), ln_g.reshape(1, d), ln_b.reshape(1, d),
      w_out_bf16, b_out.reshape(1, d), x3, g1)


ATT_T = 256


def _attn_kernel(q_ref, k_ref, v_ref, o_ref):
    t = ATT_T
    qi = pl.program_id(2)
    q2 = q_ref[0] * (1.0 / math.sqrt(HEAD_DIM))
    lane = lax.broadcasted_iota(jnp.int32, (1, LANES), 1)
    head_mask = (lane < HEAD_DIM, lane >= HEAD_DIM)
    row = lax.broadcasted_iota(jnp.int32, (t, t), 0)
    col = lax.broadcasted_iota(jnp.int32, (t, t), 1)
    tri = (row >= col).astype(BF16)
    tri2 = jnp.concatenate([tri, tri], axis=0)
    qh = tuple(jnp.where(m, q2, jnp.zeros_like(q2)) for m in head_mask)
    nt = (((1,), (1,)), ((), ()))

    def tile(kt, suf, acc, diag):
        start = pl.multiple_of(kt * t, t)
        k2 = k_ref[0, pl.ds(start, t), :]
        v2 = v_ref[0, pl.ds(start, t), :]
        new_suf = []
        for h in range(2):
            z = lax.dot_general(qh[h], k2, nt, preferred_element_type=F32)
            sp = jnp.maximum(z, 0.0) + jnp.log(1.0 + jnp.exp(-jnp.abs(z)))
            if diag:
                sp = jnp.where(col < row, sp, 0.0)
            hi = sp.astype(BF16)
            mid = (sp - hi.astype(F32)).astype(BF16)
            cs = jnp.dot(jnp.concatenate([hi, mid], axis=1), tri2,
                         preferred_element_type=F32)
            a = jnp.exp(z - cs - suf[h])
            if diag:
                a = jnp.where(col < row, a, 0.0)
            pv = jnp.dot(a.astype(BF16), v2, preferred_element_type=F32)
            acc = acc + jnp.where(head_mask[h], pv, 0.0)
            new_suf.append(suf[h] + cs[:, 0:1])
        return tuple(new_suf), acc

    zero = jnp.zeros((t, 1), F32)
    suf, acc = tile(qi, (zero, zero), jnp.zeros((t, LANES), F32), True)

    def body(n, carry):
        s0, s1, acc = carry
        (s0, s1), acc = tile(qi - n, (s0, s1), acc, False)
        return s0, s1, acc

    _, _, acc = lax.fori_loop(1, qi + 1, body, (suf[0], suf[1], acc))
    o_ref[0] = acc


def _attention(qkv3):
    bsz, seq, n3 = qkv3.shape
    d = n3 // 3
    nb = d // LANES
    return pl.pallas_call(
        _attn_kernel,
        grid=(bsz, nb, seq // ATT_T),
        in_specs=[pl.BlockSpec((1, ATT_T, LANES), lambda b, h, i: (b, i, h)),
                  pl.BlockSpec((1, seq, LANES), lambda b, h, i: (b, 0, nb + h)),
                  pl.BlockSpec((1, seq, LANES), lambda b, h, i: (b, 0, 2 * nb + h))],
        out_specs=pl.BlockSpec((1, ATT_T, LANES), lambda b, h, i: (b, i, h)),
        out_shape=jax.ShapeDtypeStruct((bsz, seq, d), F32),
        compiler_params=_cparams("parallel", "parallel", "arbitrary"),
        name="sb_attention",
    )(qkv3, qkv3, qkv3)


def _mm_res_kernel(a_ref, w_ref, x_ref, g_ref, o_ref):
    m = jnp.dot(a_ref[...].astype(BF16), w_ref[...], preferred_element_type=F32)
    o_ref[...] = x_ref[...] + g_ref[0] * m


def _mm_res(a2, w_bf16, x2, g, seq, tm=512):
    t, d = x2.shape
    k = a2.shape[1]
    return pl.pallas_call(
        _mm_res_kernel,
        grid=(t // tm,),
        in_specs=[pl.BlockSpec((tm, k), lambda i: (i, 0)),
                  pl.BlockSpec((k, d), lambda i: (0, 0)),
                  pl.BlockSpec((tm, d), lambda i: (i, 0)),
                  pl.BlockSpec((1, 1, d), lambda i: ((i * tm) // seq, 0, 0))],
        out_specs=pl.BlockSpec((tm, d), lambda i: (i, 0)),
        out_shape=jax.ShapeDtypeStruct((t, d), F32),
        compiler_params=_cparams("parallel"),
        name="mm_res",
    )(a2, w_bf16, x2, g)


PAIR_ROWS = tuple(PEER_TOPK // (i + 1) for i in range(PEER_TOPK))


def _extract_top(s, payload, k):
    rows = lax.broadcasted_iota(jnp.int32, s.shape, 0)
    big = s.shape[0]
    vals, outs = [], []
    for _ in range(k):
        m = jnp.max(s, axis=0, keepdims=True)
        r = jnp.min(jnp.where(s == m, rows, big), axis=0, keepdims=True)
        sel = rows == r
        vals.append(m)
        if payload is None:
            outs.append(r)
        else:
            outs.append(jnp.max(jnp.where(sel, payload, -1), axis=0, keepdims=True))
        s = jnp.where(sel, NEG_INF, s)
    return jnp.concatenate(vals, axis=0), jnp.concatenate(outs, axis=0)


def _pair_candidates(a, ia, b, ib):
    sub = lax.broadcasted_iota(jnp.int32, (SUBLANES, a.shape[1]), 0)
    cand, ids = [], []
    for i in range(SUBLANES):
        for j0 in range(0, PAIR_ROWS[i], SUBLANES):
            n = min(PAIR_ROWS[i] - j0, SUBLANES)
            c = a[i:i + 1, :] + b[j0:j0 + SUBLANES, :]
            e = ia[i:i + 1, :] * N_KEYS + ib[j0:j0 + SUBLANES, :]
            if n < SUBLANES:
                c = jnp.where(sub < n, c, NEG_INF)
            cand.append(c)
            ids.append(e)
    cand.append(a[SUBLANES:, :] + b[0:1, :])
    ids.append(ia[SUBLANES:, :] * N_KEYS + ib[0:1, :])
    return jnp.concatenate(cand, axis=0), jnp.concatenate(ids, axis=0)


def _peer_route_kernel(row_base, x_ref, gn_ref, sc_ref, sh_ref, wq_ref, keys_ref,
                       h_ref, e_ref, g_ref, q_s, e_s, g_s):
    h = _norm_mod(x_ref[...], gn_ref[...], sc_ref[0], sh_ref[0])
    h_ref[...] = h
    q = jnp.dot(h.astype(BF16), wq_ref[...], preferred_element_type=F32)
    for j in range(2 * PEER_HEADS):
        q_s[j] = q[:, j * PEER_DQH:(j + 1) * PEER_DQH].astype(BF16)
    nt = (((1,), (1,)), ((), ()))

    def head(hd, carry):
        tops = []
        for c in range(2):
            s_t = lax.dot_general(keys_ref[2 * hd + c], q_s[2 * hd + c], nt,
                                  preferred_element_type=F32)
            tops.append(_extract_top(s_t, None, PEER_TOPK))
        cand, ids = _pair_candidates(tops[0][0], tops[0][1], tops[1][0], tops[1][1])
        best, experts = _extract_top(cand, ids, PEER_TOPK)
        ex = jnp.exp(best - best[0:1, :])
        g_s[hd] = ex / jnp.sum(ex, axis=0, keepdims=True)
        e_s[hd] = experts
        return carry

    lax.fori_loop(0, PEER_HEADS, head, 0)
    tm = x_ref.shape[0]
    e_ref[...] = e_s[...].reshape(PEER_HK, tm).T + row_base
    g_ref[...] = g_s[...].reshape(PEER_HK, tm).T


def _peer_route(x2, seq, gn, sc, sh, wq_bf16, keys_bf16, row_base, tm=256):
    t, d = x2.shape
    nq = wq_bf16.shape[1]
    per_seq = lambda i: ((i * tm) // seq, 0, 0)
    return pl.pallas_call(
        functools.partial(_peer_route_kernel, row_base),
        grid=(t // tm,),
        in_specs=[pl.BlockSpec((tm, d), lambda i: (i, 0)),
                  pl.BlockSpec((1, d), lambda i: (0, 0)),
                  pl.BlockSpec((1, 1, d), per_seq),
                  pl.BlockSpec((1, 1, d), per_seq),
                  pl.BlockSpec((d, nq), lambda i: (0, 0)),
                  pl.BlockSpec((2 * PEER_HEADS, N_KEYS, PEER_DQH), lambda i: (0, 0, 0))],
        out_specs=[pl.BlockSpec((tm, d), lambda i: (i, 0)),
                   pl.BlockSpec((tm, PEER_HK), lambda i: (i, 0)),
                   pl.BlockSpec((tm, PEER_HK), lambda i: (i, 0))],
        out_shape=[jax.ShapeDtypeStruct((t, d), F32),
                   jax.ShapeDtypeStruct((t, PEER_HK), jnp.int32),
                   jax.ShapeDtypeStruct((t, PEER_HK), F32)],
        scratch_shapes=[pltpu.VMEM((2 * PEER_HEADS, tm, PEER_DQH), BF16),
                        pltpu.VMEM((PEER_HEADS, PEER_TOPK, tm), jnp.int32),
                        pltpu.VMEM((PEER_HEADS, PEER_TOPK, tm), F32)],
        compiler_params=_cparams("parallel"),
        name="peer_route",
    )(x2, gn.reshape(1, d), sc, sh, wq_bf16, keys_bf16)


SC_ROWS = 16
SC_GROUPS = PEER_HK // SC_ROWS
SC_TB = 32
SC_CHUNKS = D_MODEL // SC_LANES
SC_HALF = D_MODEL // 2
SC_WCH = SC_HALF // SC_LANES
SC_NBUF = 8


def _pack_table(tab):
    b = lax.bitcast_convert_type(tab.reshape(-1, D_MODEL).astype(BF16), jnp.uint16)
    b = b.astype(jnp.uint32)
    return lax.bitcast_convert_type(b[:, :SC_HALF] | (b[:, SC_HALF:] << 16), jnp.int32)


def _unpack_pair(w):
    lo = plsc.bitcast(lax.shift_left(w, jnp.int32(16)), F32)
    hi = plsc.bitcast(w & jnp.int32(-65536), F32)
    return lo, hi


SC_BF16_GROUP = 4


def _group_products(ws, xs):
    prods = [plsc.bitcast(w, BF16) * x for w, x in zip(ws, xs)]
    return _unpack_pair(plsc.bitcast(_tree_sum(prods), jnp.int32))


def _sc_mesh():
    return plsc.VectorSubcoreMesh(core_axis_name="c", subcore_axis_name="s")


def _sc_pipeline(n_items, gather, compute):
    ahead = SC_NBUF - 1
    for i in range(ahead):
        gather(i, i).start()

    @pl.loop(0, n_items, step=SC_NBUF)
    def _(it):
        for b in range(SC_NBUF):
            i = it + b
            gather(i, b).wait()

            @pl.when(i + ahead < n_items)
            def _():
                gather(i + ahead, (b + ahead) % SC_NBUF).start()

            compute(i, b)


def _tree_sum(terms):
    while len(terms) > 1:
        terms = [terms[i] + terms[i + 1] for i in range(0, len(terms), 2)]
    return terms[0]


def _peer_pre(h_pk, idx, u_tab):
    t = h_pk.shape[0]
    info = plsc.get_sparse_core_info()
    nc, nw = info.num_cores, info.num_cores * info.num_subcores
    tpw = t // nw
    n_items = SC_TB * SC_GROUPS

    @functools.partial(
        pl.kernel, mesh=_sc_mesh(),
        out_type=jax.ShapeDtypeStruct((t, PEER_HK), F32),
        scratch_types=[pltpu.VMEM((SC_TB, PEER_HK), jnp.int32),
                       pltpu.VMEM((SC_TB, SC_HALF), jnp.int32),
                       pltpu.VMEM((SC_TB, PEER_HK), F32),
                       pltpu.VMEM((SC_ROWS, SC_LANES), F32)]
                      + [pltpu.VMEM((SC_ROWS, SC_HALF), jnp.int32)] * SC_NBUF
                      + [pltpu.SemaphoreType.DMA] * SC_NBUF,
        compiler_params=pltpu.CompilerParams(needs_layout_passes=False),
        name="peer_pre")
    def k(h_hbm, idx_hbm, u_hbm, p_hbm, idx_v, h_v, p_v, tr_v, *ring):
        wid = lax.axis_index("s") * nc + lax.axis_index("c")
        bufs, sems = ring[:SC_NBUF], ring[SC_NBUF:]
        lane = lax.iota(jnp.int32, SC_LANES)

        def gather(it, b):
            tl, g = it // SC_GROUPS, it % SC_GROUPS
            return pltpu.make_async_copy(
                u_hbm.at[idx_v.at[tl, pl.ds(g * SC_ROWS, SC_ROWS)]], bufs[b], sems[b])

        def compute(it, b):
            tl, g = it // SC_GROUPS, it % SC_GROUPS
            buf = bufs[b]

            def body(c, accs):
                sls = [pl.ds((c * SC_BF16_GROUP + q) * SC_LANES, SC_LANES)
                       for q in range(SC_BF16_GROUP)]
                xs = [plsc.bitcast(h_v[tl, sl], BF16) for sl in sls]
                out = []
                for r in range(SC_ROWS):
                    lo, hi = _group_products([buf[r, sl] for sl in sls], xs)
                    out.append(accs[r] + (lo + hi))
                return tuple(out)

            zero = jnp.zeros((SC_LANES,), F32)
            accs = lax.fori_loop(0, SC_WCH // SC_BF16_GROUP, body, (zero,) * SC_ROWS)
            skew = [(lane + j) & (SC_LANES - 1) for j in range(SC_LANES)]
            for r in range(SC_ROWS):
                plsc.store_scatter(tr_v, [jnp.full((SC_LANES,), r, jnp.int32), skew[r]], accs[r])
            cols = [plsc.load_gather(tr_v, [lane, skew[j]]) for j in range(SC_LANES)]
            p_v[tl, pl.ds(g * SC_ROWS, SC_ROWS)] = _tree_sum(cols)

        @pl.loop(0, tpw // SC_TB)
        def _(blk):
            t0 = wid * tpw + blk * SC_TB
            pltpu.sync_copy(idx_hbm.at[pl.ds(t0, SC_TB)], idx_v)
            pltpu.sync_copy(h_hbm.at[pl.ds(t0, SC_TB)], h_v)
            _sc_pipeline(n_items, gather, compute)
            pltpu.sync_copy(p_v, p_hbm.at[pl.ds(t0, SC_TB)])

    return k(h_pk, idx, u_tab)


def _peer_post(act, idx, v_tab):
    t = act.shape[0]
    d = 2 * v_tab.shape[1]
    info = plsc.get_sparse_core_info()
    nc, nw = info.num_cores, info.num_cores * info.num_subcores
    tpw = t // nw
    n_items = SC_TB * SC_GROUPS

    @functools.partial(
        pl.kernel, mesh=_sc_mesh(),
        out_type=jax.ShapeDtypeStruct((t, d), F32),
        scratch_types=[pltpu.VMEM((SC_TB, PEER_HK), jnp.int32),
                       pltpu.VMEM((SC_TB, PEER_HK), jnp.int32),
                       pltpu.VMEM((SC_TB, d), F32)]
                      + [pltpu.VMEM((SC_ROWS, SC_HALF), jnp.int32)] * SC_NBUF
                      + [pltpu.SemaphoreType.DMA] * SC_NBUF,
        compiler_params=pltpu.CompilerParams(needs_layout_passes=False),
        name="peer_post")
    def k(a_hbm, idx_hbm, v_hbm, y_hbm, idx_v, a_v, y_v, *ring):
        wid = lax.axis_index("s") * nc + lax.axis_index("c")
        bufs, sems = ring[:SC_NBUF], ring[SC_NBUF:]

        def gather(it, b):
            tl, g = it // SC_GROUPS, it % SC_GROUPS
            return pltpu.make_async_copy(
                v_hbm.at[idx_v.at[tl, pl.ds(g * SC_ROWS, SC_ROWS)]], bufs[b], sems[b])

        def compute(it, b):
            tl, g = it // SC_GROUPS, it % SC_GROUPS
            buf = bufs[b]
            tlv = jnp.full((SC_LANES,), tl, jnp.int32)
            w = tuple(plsc.bitcast(plsc.load_gather(
                a_v, [tlv, jnp.full((SC_LANES,), g * SC_ROWS + r, jnp.int32)]), BF16)
                for r in range(SC_ROWS))

            @plsc.parallel_loop(0, SC_WCH, unroll=2)
            def _(c):
                sl = pl.ds(c * SC_LANES, SC_LANES)
                parts = [_group_products([buf[r, sl] for r in range(g0, g0 + SC_BF16_GROUP)],
                                         w[g0:g0 + SC_BF16_GROUP])
                         for g0 in range(0, SC_ROWS, SC_BF16_GROUP)]
                plsc.addupdate(y_v.at[tl, sl], _tree_sum([p[0] for p in parts]))
                plsc.addupdate(y_v.at[tl, pl.ds(SC_HALF + c * SC_LANES, SC_LANES)],
                               _tree_sum([p[1] for p in parts]))

        @pl.loop(0, tpw // SC_TB)
        def _(blk):
            t0 = wid * tpw + blk * SC_TB
            pltpu.sync_copy(idx_hbm.at[pl.ds(t0, SC_TB)], idx_v)
            pltpu.sync_copy(a_hbm.at[pl.ds(t0, SC_TB)], a_v)

            @pl.loop(0, SC_TB)
            def _(r):
                @plsc.parallel_loop(0, SC_CHUNKS, unroll=4)
                def _(c):
                    y_v[r, pl.ds(c * SC_LANES, SC_LANES)] = jnp.zeros((SC_LANES,), F32)

            _sc_pipeline(n_items, gather, compute)
            pltpu.sync_copy(y_v, y_hbm.at[pl.ds(t0, SC_TB)])

    return k(act, idx, v_tab)


def _act_kernel(p_ref, g_ref, o_ref):
    p = p_ref[...]
    o_ref[...] = (p * (lax.erf(p * (1.0 / math.sqrt(2.0))) + 1.0) * 0.5) * g_ref[...]


def _expert_act(p, gates, tm=512):
    t, n = p.shape
    spec = pl.BlockSpec((tm, n), lambda i: (i, 0))
    return pl.pallas_call(
        _act_kernel, grid=(t // tm,), in_specs=[spec, spec], out_specs=spec,
        out_shape=jax.ShapeDtypeStruct((t, n), F32),
        compiler_params=_cparams("parallel"), name="expert_act",
    )(p, gates)


def _res_kernel(x_ref, y_ref, g_ref, o_ref):
    o_ref[...] = x_ref[...] + g_ref[0] * y_ref[...]


def _res_norm_kernel(x_ref, y_ref, g_ref, fg_ref, o_ref):
    x = x_ref[...] + g_ref[0] * y_ref[...]
    ms = jnp.mean(x * x, axis=-1, keepdims=True)
    o_ref[...] = x * lax.rsqrt(ms + EPS) * fg_ref[...]


def _residual(x2, y2, g, seq, final_g=None, tm=512):
    t, d = x2.shape
    tile = pl.BlockSpec((tm, d), lambda i: (i, 0))
    gspec = pl.BlockSpec((1, 1, d), lambda i: ((i * tm) // seq, 0, 0))
    if final_g is None:
        kern, ins, args = _res_kernel, [tile, tile, gspec], (x2, y2, g)
    else:
        kern = _res_norm_kernel
        ins = [tile, tile, gspec, pl.BlockSpec((1, d), lambda i: (0, 0))]
        args = (x2, y2, g, final_g.reshape(1, d))
    return pl.pallas_call(
        kern, grid=(t // tm,), in_specs=ins, out_specs=tile,
        out_shape=jax.ShapeDtypeStruct((t, d), F32),
        compiler_params=_cparams("parallel"), name="residual",
    )(*args)


def _pack_same(a):
    b = lax.bitcast_convert_type(a.astype(BF16), jnp.uint16).astype(jnp.uint32)
    return lax.bitcast_convert_type(b | (b << 16), jnp.int32)


def _peer(x2, seq, gn, sc, sh, wq_bf16, keys_bf16, u_tab, v_tab):
    h2, experts, gates = _peer_route(x2, seq, gn, sc, sh, wq_bf16, keys_bf16, 0)
    p = _peer_pre(_pack_table(h2), experts, u_tab)
    act = _expert_act(p, gates)
    return _peer_post(_pack_same(act), experts, v_tab)


BATCH_SPLIT = 16


def kernel(x, c, ada_w, ada_b, norm_mix_g, norm_ffn_g, cv_w_in, cv_b_in, cv_w_dw, cv_b_dw, cv_ln_g, cv_ln_b, cv_w_out, cv_b_out, sb_w_qkv, sb_w_o, pk_w_q, pk_keys, pk_u, pk_v, final_g):
    bsz, seq, d = x.shape
    mod = _ada_mod(c, ada_w, ada_b)
    w_in, w_out = cv_w_in.astype(BF16), cv_w_out.astype(BF16)
    w_qkv, w_o = sb_w_qkv.astype(BF16), sb_w_o.astype(BF16)
    w_q = pk_w_q.astype(BF16)
    keys = pk_keys.reshape(DEPTH, 2 * PEER_HEADS, N_KEYS, PEER_DQH).astype(BF16)
    u_tabs = [_pack_table(pk_u[i]) for i in range(DEPTH)]
    v_tabs = [_pack_table(pk_v[i]) for i in range(DEPTH)]
    nb = bsz // BATCH_SPLIT
    t = nb * seq
    outs = []
    for part in range(BATCH_SPLIT):
        lo = part * nb
        x2 = None
        for i in range(DEPTH):
            full = [mod[i, :, n * d:(n + 1) * d].reshape(bsz, 1, d) for n in range(ADA_CHUNKS)]
            sh1, sc1, g1, sh2, sc2, g2 = (m[lo:lo + nb] for m in full)
            j = i // 2
            if i % 2 == 0:
                if i == 0:
                    xin, scv, shv, gv, seq0 = x, full[1], full[0], full[2], lo
                else:
                    xin, scv, shv, gv, seq0 = x2.reshape(nb, seq, d), sc1, sh1, g1, 0
                u2 = _norm_glu(xin.reshape(-1, d), seq, norm_mix_g[i], scv, shv,
                               w_in[j], cv_b_in[j], seq0, nb)
                x2 = _conv_block(u2.reshape(nb, seq, d), xin, gv,
                                 cv_w_dw[j], cv_b_dw[j], cv_ln_g[j], cv_ln_b[j],
                                 w_out[j], cv_b_out[j], seq0).reshape(t, d)
            else:
                qkv = _norm_mm(x2, seq, norm_mix_g[i], sc1, sh1, w_qkv[j], BF16)
                o = _attention(qkv.reshape(nb, seq, 3 * d))
                x2 = _mm_res(o.reshape(t, d), w_o[j], x2, g1, seq)
            y2 = _peer(x2, seq, norm_ffn_g[i], sc2, sh2, w_q[i], keys[i], u_tabs[i], v_tabs[i])
            x2 = _residual(x2, y2, g2, seq, final_g if i == DEPTH - 1 else None)
        outs.append(x2.reshape(nb, seq, d))
    return jnp.concatenate(outs, axis=0)
```
